```python
import math
import jax, jax.numpy as jnp
from jax import lax
import numpy as np

D_MODEL = 1024
BATCH = 16
SEQ = 2048
DEPTH = 1

MIX_WIDTH = D_MODEL
POOL_WIDTH = MIX_WIDTH // 2
POOL_GROUPS = 4
POOL_GROUP_DIM = POOL_WIDTH // POOL_GROUPS
POOL_WINDOWS = (2, 4, 8, 16)
ATTN_WIDTH = MIX_WIDTH - POOL_WIDTH
HEAD_DIM = 64
N_HEADS = ATTN_WIDTH // HEAD_DIM
D_FF = 2816
Q_BLOCK = 128
IN_COLS = POOL_WIDTH + 3 * ATTN_WIDTH + N_HEADS
EPS = 1e-6

kernel_name = "hymba_pool_fox_macaron_block"


def rmsnorm(x, g):
    xf = x.astype(jnp.float32)
    y = xf * lax.rsqrt(jnp.mean(xf * xf, axis=-1, keepdims=True) + EPS)
    return (y * g.astype(jnp.float32)).astype(x.dtype)


def swiglu(h, w_gate, w_up, w_down):
    return (jax.nn.silu(h @ w_gate) * (h @ w_up)) @ w_down


def causal_window_mean(v, w):
    B, S, C = v.shape
    vf = v.astype(jnp.float32)
    cs = jnp.cumsum(vf, axis=1)
    shifted = jnp.concatenate([jnp.zeros((B, w, C), jnp.float32), cs[:, : S - w]], axis=1)
    count = jnp.minimum(jnp.arange(1, S + 1, dtype=jnp.float32), float(w))
    return ((cs - shifted) / count[None, :, None]).astype(v.dtype)


def pool_mixer(pv, pool_w, pool_scale):
    B, S, _ = pv.shape
    groups = pv.reshape(B, S, POOL_GROUPS, POOL_GROUP_DIM)
    pooled = jnp.stack(
        [causal_window_mean(groups[:, :, g], POOL_WINDOWS[g]) for g in range(POOL_GROUPS)], axis=2
    ) - groups
    mixed = jnp.einsum("bsgc,gcd->bsgd", pooled, pool_w)
    return mixed.reshape(B, S, POOL_WIDTH) * pool_scale


def forgetting_attention(q, k, v, f_logit, b_forget, q_norm, k_norm):
    B, S, H, Dh = q.shape
    q = rmsnorm(q, q_norm).transpose(0, 2, 1, 3)
    k = rmsnorm(k, k_norm).transpose(0, 2, 1, 3)
    v = v.transpose(0, 2, 1, 3)
    log_f = jax.nn.log_sigmoid((f_logit + b_forget).astype(jnp.float32))
    F = jnp.cumsum(log_f, axis=1).transpose(0, 2, 1)
    scale = 1.0 / math.sqrt(Dh)
    outs = []
    for i in range(S // Q_BLOCK):
        q0, end = i * Q_BLOCK, (i + 1) * Q_BLOCK
        qb = q[:, :, q0:end]
        kb, vb = k[:, :, :end], v[:, :, :end]
        logits = jnp.einsum("bhqd,bhkd->bhqk", qb, kb).astype(jnp.float32) * scale
        logits = logits + F[:, :, q0:end, None] - F[:, :, None, :end]
        q_pos = jnp.arange(q0, end)[:, None]
        k_pos = jnp.arange(end)[None, :]
        logits = jnp.where(q_pos >= k_pos, logits, -jnp.inf)
        p = jax.nn.softmax(logits, axis=-1).astype(vb.dtype)
        outs.append(jnp.einsum("bhqk,bhkd->bhqd", p, vb))
    o = jnp.concatenate(outs, axis=2)
    return o.transpose(0, 2, 1, 3).reshape(B, S, H * Dh)


def setup_inputs(seed: int = 0) -> dict:
    key = jax.random.key(seed)
    ks = jax.random.split(key, 24)
    f32 = jnp.float32

    def nrm(k, shape, fan_in):
        return jax.random.normal(k, shape, f32) * fan_in ** -0.5

    def gain(k, shape):
        return 1.0 + 0.02 * jax.random.normal(k, shape, f32)

    return {
        "x": jax.random.normal(ks[0], (BATCH, SEQ, D_MODEL), f32),
        "ffn1_norm": gain(ks[1], (D_MODEL,)),
        "ffn1_w_gate": nrm(ks[2], (D_MODEL, D_FF), D_MODEL),
        "ffn1_w_up": nrm(ks[3], (D_MODEL, D_FF), D_MODEL),
        "ffn1_w_down": nrm(ks[4], (D_FF, D_MODEL), D_FF),
        "mix_norm": gain(ks[5], (D_MODEL,)),
        "w_in": nrm(ks[6], (D_MODEL, IN_COLS), D_MODEL),
        "b_forget": jax.random.uniform(ks[7], (N_HEADS,), f32, 1.0, 4.0),
        "pool_w": nrm(ks[8], (POOL_GROUPS, POOL_GROUP_DIM, POOL_GROUP_DIM), POOL_GROUP_DIM),
        "pool_scale": gain(ks[9], (POOL_WIDTH,)),
        "q_norm": gain(ks[10], (HEAD_DIM,)),
        "k_norm": gain(ks[11], (HEAD_DIM,)),
        "out_norm_pool": gain(ks[12], (POOL_WIDTH,)),
        "out_norm_attn": gain(ks[13], (ATTN_WIDTH,)),
        "w_out": nrm(ks[14], (MIX_WIDTH, D_MODEL), MIX_WIDTH),
        "ffn2_norm": gain(ks[15], (D_MODEL,)),
        "ffn2_w_gate": nrm(ks[16], (D_MODEL, D_FF), D_MODEL),
        "ffn2_w_up": nrm(ks[17], (D_MODEL, D_FF), D_MODEL),
        "ffn2_w_down": nrm(ks[18], (D_FF, D_MODEL), D_FF),
    }


def reference(x, ffn1_norm, ffn1_w_gate, ffn1_w_up, ffn1_w_down, mix_norm, w_in, b_forget,
              pool_w, pool_scale, q_norm, k_norm, out_norm_pool, out_norm_attn, w_out,
              ffn2_norm, ffn2_w_gate, ffn2_w_up, ffn2_w_down):
    B, S, _ = x.shape
    for _layer in range(DEPTH):
        x = x + 0.5 * swiglu(rmsnorm(x, ffn1_norm), ffn1_w_gate, ffn1_w_up, ffn1_w_down)

        h = rmsnorm(x, mix_norm) @ w_in
        c0 = POOL_WIDTH
        pv = h[..., :c0]
        q = h[..., c0:c0 + ATTN_WIDTH].reshape(B, S, N_HEADS, HEAD_DIM)
        k = h[..., c0 + ATTN_WIDTH:c0 + 2 * ATTN_WIDTH].reshape(B, S, N_HEADS, HEAD_DIM)
        v = h[..., c0 + 2 * ATTN_WIDTH:c0 + 3 * ATTN_WIDTH].reshape(B, S, N_HEADS, HEAD_DIM)
        f_logit = h[..., c0 + 3 * ATTN_WIDTH:]

        y_pool = rmsnorm(pool_mixer(pv, pool_w, pool_scale), out_norm_pool)
        y_attn = rmsnorm(forgetting_attention(q, k, v, f_logit, b_forget, q_norm, k_norm), out_norm_attn)
        x = x + jnp.concatenate([y_pool, y_attn], axis=-1) @ w_out

        x = x + 0.5 * swiglu(rmsnorm(x, ffn2_norm), ffn2_w_gate, ffn2_w_up, ffn2_w_down)
    return x
```

```python
import functools
import math

import jax
import jax.numpy as jnp
from jax import lax
from jax.experimental import pallas as pl
from jax.experimental.pallas import tpu as pltpu

D_MODEL = 1024
D_FF = 2816
POOL_WIDTH = 512
POOL_GROUPS = 4
POOL_GROUP_DIM = 128
POOL_WINDOWS = (2, 4, 8, 16)
ATTN_WIDTH = 512
HEAD_DIM = 64
N_HEADS = 8
EPS = 1e-6
LOG2E = math.log2(math.e)

LANES = 128
HEADS_PER_BLOCK = LANES // HEAD_DIM
N_HEAD_BLOCKS = N_HEADS // HEADS_PER_BLOCK
POOL_HALO = 16

TOKEN_TILE = 512
FF_CHUNK = 256
ATTN_TILE = 256
VMEM_LIMIT = 56 * 1024 * 1024
MASK_VALUE = -1e30

bf16 = jnp.bfloat16
f32 = jnp.float32


def _rms_scale(x):
    return lax.rsqrt(jnp.mean(x * x, axis=-1, keepdims=True) + EPS)


def _swiglu_residual(x, norm_w, wg_ref, wu_ref, wd_ref):
    h = (x * _rms_scale(x) * norm_w).astype(bf16)
    acc = jnp.zeros(x.shape, f32)
    for c in range(D_FF // FF_CHUNK):
        cols = slice(c * FF_CHUNK, (c + 1) * FF_CHUNK)
        g = jnp.dot(h, wg_ref[:, cols], preferred_element_type=f32)
        u = jnp.dot(h, wu_ref[:, cols], preferred_element_type=f32)
        a = (g * jax.nn.sigmoid(g) * u).astype(bf16)
        acc = acc + jnp.dot(a, wd_ref[cols, :], preferred_element_type=f32)
    return x + 0.5 * acc


def _ffn1_kernel(x_ref, nw_ref, wg_ref, wu_ref, wd_ref, o_ref):
    o_ref[...] = _swiglu_residual(x_ref[...], nw_ref[...], wg_ref, wu_ref, wd_ref)


def _mix_ffn2_kernel(x_ref, yp_ref, oa_ref, nattn_ref, wo_ref, nw_ref, wg_ref, wu_ref, wd_ref, o_ref):
    oa = oa_ref[...]
    ya = (oa * _rms_scale(oa) * nattn_ref[...]).astype(bf16)
    x2 = (x_ref[...]
          + jnp.dot(yp_ref[...], wo_ref[:POOL_WIDTH, :], preferred_element_type=f32)
          + jnp.dot(ya, wo_ref[POOL_WIDTH:, :], preferred_element_type=f32))
    o_ref[...] = _swiglu_residual(x2, nw_ref[...], wg_ref, wu_ref, wd_ref)


def _split3_bf16(x):
    hi = x.astype(bf16)
    r = x - hi.astype(f32)
    mid = r.astype(bf16)
    lo = (r - mid.astype(f32)).astype(bf16)
    return hi, mid, lo


def _in_proj_kernel(x_ref, nw_ref, wpv_ref, wq_ref, wk_ref, wv_ref, wf_ref, bf_ref,
                    poolw_ref, pscale_ref, npool_ref, qn_ref, kn_ref, headsum_ref, tri_ref,
                    yp_ref, q_ref, k_ref, v_ref, fcum_ref,
                    halo_ref, carry_ref):
    s_idx = pl.program_id(1)
    tm = x_ref.shape[1]

    @pl.when(s_idx == 0)
    def _():
        halo_ref[...] = jnp.zeros(halo_ref.shape, f32)
        carry_ref[...] = jnp.zeros(carry_ref.shape, f32)

    x = x_ref[0]
    h = (x * _rms_scale(x) * nw_ref[...]).astype(bf16)

    def head_norm(w_ref, gain_ref):
        t = jnp.dot(h, w_ref[...], preferred_element_type=f32)
        ssq = jnp.dot((t * t).astype(bf16), headsum_ref[...], preferred_element_type=f32)
        return (t * lax.rsqrt(ssq * (1.0 / HEAD_DIM) + EPS) * gain_ref[...]).astype(bf16)

    q_ref[0] = head_norm(wq_ref, qn_ref)
    k_ref[0] = head_norm(wk_ref, kn_ref)
    v_ref[0] = jnp.dot(h, wv_ref[...], preferred_element_type=f32).astype(bf16)

    fl = jnp.dot(h, wf_ref[...], preferred_element_type=f32) + bf_ref[...]
    logf = jnp.minimum(fl, 0.0) - jnp.log1p(jnp.exp(-jnp.abs(fl)))
    hi, mid, lo = _split3_bf16(logf)
    tri = tri_ref[...]
    fcum = (jnp.dot(tri, hi, preferred_element_type=f32)
            + jnp.dot(tri, mid, preferred_element_type=f32)
            + jnp.dot(tri, lo, preferred_element_type=f32)
            + carry_ref[...])
    carry_ref[...] = fcum[tm - 1:tm, :]
    fcum_ref[0] = fcum * LOG2E

    pv = jnp.dot(h, wpv_ref[...], preferred_element_type=f32)
    full = jnp.concatenate([halo_ref[...], pv], axis=0)
    halo_ref[...] = pv[tm - POOL_HALO:, :]
    pos = s_idx * tm + lax.broadcasted_iota(jnp.int32, (tm, 1), 0)
    mixed = []
    for g in range(POOL_GROUPS):
        w = POOL_WINDOWS[g]
        cols = slice(g * POOL_GROUP_DIM, (g + 1) * POOL_GROUP_DIM)
        win = full[:, cols]
        shift = 1
        while shift < w:
            win = win + pltpu.roll(win, shift, 0)
            shift *= 2
        count = jnp.minimum(pos + 1, w).astype(f32)
        pooled = win[POOL_HALO:, :] / count - pv[:, cols]
        mixed.append(jnp.dot(pooled.astype(bf16), poolw_ref[g], preferred_element_type=f32))
    mixed = jnp.concatenate(mixed, axis=1) * pscale_ref[...]
    yp_ref[0] = (mixed * _rms_scale(mixed) * npool_ref[...]).astype(bf16)


def _attn_kernel(q_ref, k_ref, v_ref, frow_ref, fcol_ref, o_ref, m_ref, l_ref, acc_ref):
    i = pl.program_id(2)
    tq = q_ref.shape[1]
    tk = tq
    q2 = q_ref[0]
    lane = lax.broadcasted_iota(jnp.int32, (tq, LANES), 1)
    zero = jnp.zeros_like(q2)
    q_heads = [jnp.where(lane < HEAD_DIM, q2, zero), jnp.where(lane >= HEAD_DIM, q2, zero)]
    fq = [fcol_ref[0, 0, :, hh:hh + 1] for hh in range(HEADS_PER_BLOCK)]

    m_ref[...] = jnp.full(m_ref.shape, MASK_VALUE, f32)
    l_ref[...] = jnp.zeros(l_ref.shape, f32)
    acc_ref[...] = jnp.zeros(acc_ref.shape, f32)

    def block(j, masked):
        start = pl.multiple_of(j * tk, tk)
        kb = k_ref[0, pl.ds(start, tk), :]
        vb = v_ref[0, pl.ds(start, tk), :]
        for hh in range(HEADS_PER_BLOCK):
            s = lax.dot_general(q_heads[hh], kb, (((1,), (1,)), ((), ())), preferred_element_type=f32)
            fk = frow_ref[0, 0, hh:hh + 1, pl.ds(start, tk)]
            s = s + (fq[hh] - fk)
            if masked:
                row = lax.broadcasted_iota(jnp.int32, (tq, tk), 0)
                col = lax.broadcasted_iota(jnp.int32, (tq, tk), 1)
                s = jnp.where(row >= col, s, MASK_VALUE)
            m_prev = m_ref[hh]
            m_new = jnp.maximum(m_prev, jnp.max(s, axis=-1, keepdims=True))
            alpha = jnp.exp2(m_prev - m_new)
            p = jnp.exp2(s - m_new)
            l_ref[hh] = alpha * l_ref[hh] + jnp.sum(p, axis=-1, keepdims=True)
            acc_ref[hh] = alpha * acc_ref[hh] + jnp.dot(p.astype(bf16), vb, preferred_element_type=f32)
            m_ref[hh] = m_new

    def body(j, carry):
        block(j, masked=False)
        return carry

    lax.fori_loop(0, i, body, 0)
    block(i, masked=True)

    o0 = acc_ref[0] / l_ref[0]
    o1 = acc_ref[1] / l_ref[1]
    o_ref[0] = jnp.where(lane < HEAD_DIM, o0, o1)


def _resident(shape):
    nd = len(shape)
    return pl.BlockSpec(shape, lambda *_: (0,) * nd, pipeline_mode=pl.Buffered(1))


def _ffn_weight_specs():
    return [_resident((1, D_MODEL)), _resident((D_MODEL, D_FF)), _resident((D_MODEL, D_FF)),
            _resident((D_FF, D_MODEL))]


def kernel(x, ffn1_norm, ffn1_w_gate, ffn1_w_up, ffn1_w_down, mix_norm, w_in, b_forget, pool_w, pool_scale, q_norm, k_norm, out_norm_pool, out_norm_attn, w_out, ffn2_norm, ffn2_w_gate, ffn2_w_up, ffn2_w_down):
    B, S, D = x.shape
    assert D == D_MODEL and S % TOKEN_TILE == 0 and S % ATTN_TILE == 0
    T = B * S
    n_tok = T // TOKEN_TILE
    row = lambda a: a.reshape(1, -1).astype(f32)

    tok_spec = pl.BlockSpec((TOKEN_TILE, D_MODEL), lambda t: (t, 0))
    params_1d = pltpu.CompilerParams(dimension_semantics=("arbitrary",), vmem_limit_bytes=VMEM_LIMIT)

    x1 = pl.pallas_call(
        _ffn1_kernel,
        grid=(n_tok,),
        in_specs=[tok_spec] + _ffn_weight_specs(),
        out_specs=tok_spec,
        out_shape=jax.ShapeDtypeStruct((T, D_MODEL), f32),
        compiler_params=params_1d,
        name="ffn1",
    )(x.reshape(T, D_MODEL), row(ffn1_norm), ffn1_w_gate.astype(bf16), ffn1_w_up.astype(bf16),
      ffn1_w_down.astype(bf16))

    c0 = POOL_WIDTH
    w_pv = w_in[:, :c0].astype(bf16)
    w_q = w_in[:, c0:c0 + ATTN_WIDTH].astype(bf16)
    w_k = w_in[:, c0 + ATTN_WIDTH:c0 + 2 * ATTN_WIDTH].astype(bf16)
    w_v = w_in[:, c0 + 2 * ATTN_WIDTH:c0 + 3 * ATTN_WIDTH].astype(bf16)
    w_f = jnp.pad(w_in[:, c0 + 3 * ATTN_WIDTH:], ((0, 0), (0, LANES - N_HEADS))).astype(bf16)
    b_f = jnp.pad(b_forget.astype(f32), (0, LANES - N_HEADS)).reshape(1, LANES)
    q_gain = row(jnp.tile(q_norm, N_HEADS)) * (LOG2E / math.sqrt(HEAD_DIM))
    k_gain = row(jnp.tile(k_norm, N_HEADS))
    head_id = jnp.arange(ATTN_WIDTH) // HEAD_DIM
    headsum = (head_id[:, None] == head_id[None, :]).astype(bf16)
    tri = (jnp.arange(TOKEN_TILE)[:, None] >= jnp.arange(TOKEN_TILE)[None, :]).astype(bf16)

    seq_spec = lambda width: pl.BlockSpec((1, TOKEN_TILE, width), lambda b, s: (b, s, 0))
    y_pool, q, k, v, fcum = pl.pallas_call(
        _in_proj_kernel,
        grid=(B, S // TOKEN_TILE),
        in_specs=[seq_spec(D_MODEL), _resident((1, D_MODEL)),
                  _resident((D_MODEL, POOL_WIDTH)), _resident((D_MODEL, ATTN_WIDTH)),
                  _resident((D_MODEL, ATTN_WIDTH)), _resident((D_MODEL, ATTN_WIDTH)),
                  _resident((D_MODEL, LANES)), _resident((1, LANES)),
                  _resident((POOL_GROUPS, POOL_GROUP_DIM, POOL_GROUP_DIM)), _resident((1, POOL_WIDTH)),
                  _resident((1, POOL_WIDTH)), _resident((1, ATTN_WIDTH)), _resident((1, ATTN_WIDTH)),
                  _resident((ATTN_WIDTH, ATTN_WIDTH)), _resident((TOKEN_TILE, TOKEN_TILE))],
        out_specs=[seq_spec(POOL_WIDTH), seq_spec(ATTN_WIDTH), seq_spec(ATTN_WIDTH), seq_spec(ATTN_WIDTH),
                   seq_spec(LANES)],
        out_shape=[jax.ShapeDtypeStruct((B, S, POOL_WIDTH), bf16),
                   jax.ShapeDtypeStruct((B, S, ATTN_WIDTH), bf16),
                   jax.ShapeDtypeStruct((B, S, ATTN_WIDTH), bf16),
                   jax.ShapeDtypeStruct((B, S, ATTN_WIDTH), bf16),
                   jax.ShapeDtypeStruct((B, S, LANES), f32)],
        scratch_shapes=[pltpu.VMEM((POOL_HALO, POOL_WIDTH), f32), pltpu.VMEM((1, LANES), f32)],
        compiler_params=pltpu.CompilerParams(dimension_semantics=("arbitrary", "arbitrary"),
                                             vmem_limit_bytes=VMEM_LIMIT),
        name="in_proj",
    )(x1.reshape(B, S, D_MODEL), row(mix_norm), w_pv, w_q, w_k, w_v, w_f, b_f,
      pool_w.astype(bf16), row(pool_scale), row(out_norm_pool), q_gain, k_gain, headsum, tri)

    f2 = fcum[:, :, :N_HEADS]
    f_col = f2.reshape(B, S, N_HEAD_BLOCKS, HEADS_PER_BLOCK).transpose(0, 2, 1, 3)
    f_row = f_col.transpose(0, 1, 3, 2)
    qo_spec = pl.BlockSpec((1, ATTN_TILE, LANES), lambda b, p, i: (b, i, p))
    kv_spec = pl.BlockSpec((1, S, LANES), lambda b, p, i: (b, 0, p))
    o_attn = pl.pallas_call(
        _attn_kernel,
        grid=(B, N_HEAD_BLOCKS, S // ATTN_TILE),
        in_specs=[qo_spec, kv_spec, kv_spec,
                  pl.BlockSpec((1, 1, HEADS_PER_BLOCK, S), lambda b, p, i: (b, p, 0, 0)),
                  pl.BlockSpec((1, 1, ATTN_TILE, HEADS_PER_BLOCK), lambda b, p, i: (b, p, i, 0))],
        out_specs=qo_spec,
        out_shape=jax.ShapeDtypeStruct((B, S, ATTN_WIDTH), f32),
        scratch_shapes=[pltpu.VMEM((HEADS_PER_BLOCK, ATTN_TILE, 1), f32),
                        pltpu.VMEM((HEADS_PER_BLOCK, ATTN_TILE, 1), f32),
                        pltpu.VMEM((HEADS_PER_BLOCK, ATTN_TILE, LANES), f32)],
        compiler_params=pltpu.CompilerParams(dimension_semantics=("arbitrary", "arbitrary", "arbitrary"),
                                             vmem_limit_bytes=VMEM_LIMIT),
        name="attention",
    )(q, k, v, f_row, f_col)

    half_spec = pl.BlockSpec((TOKEN_TILE, POOL_WIDTH), lambda t: (t, 0))
    out = pl.pallas_call(
        _mix_ffn2_kernel,
        grid=(n_tok,),
        in_specs=[tok_spec, half_spec, half_spec, _resident((1, ATTN_WIDTH)),
                  _resident((D_MODEL, D_MODEL))] + _ffn_weight_specs(),
        out_specs=tok_spec,
        out_shape=jax.ShapeDtypeStruct((T, D_MODEL), f32),
        compiler_params=params_1d,
        name="mix_ffn2",
    )(x1, y_pool.reshape(T, POOL_WIDTH), o_attn.reshape(T, ATTN_WIDTH), row(out_norm_attn),
      w_out.astype(bf16), row(ffn2_norm), ffn2_w_gate.astype(bf16), ffn2_w_up.astype(bf16),
      ffn2_w_down.astype(bf16))
    return out.reshape(B, S, D_MODEL)
```

```python
import math

import numpy as np
import jax
import jax.numpy as jnp
from jax import lax
from jax.experimental import pallas as pl
from jax.experimental.pallas import tpu as pltpu

D_MODEL = 1024
D_FF = 2816
POOL_WIDTH = 512
POOL_GROUPS = 4
POOL_GROUP_DIM = 128
POOL_WINDOWS = (2, 4, 8, 16)
ATTN_WIDTH = 512
HEAD_DIM = 64
N_HEADS = 8
EPS = 1e-6
LOG2E = math.log2(math.e)

LANES = 128
BF16_ROWS = 16
HEADS_PER_BLOCK = LANES // HEAD_DIM
N_HEAD_BLOCKS = N_HEADS // HEADS_PER_BLOCK
POOL_HALO = 16
N_SPLIT = 3
AUG_STRIDE = 2 * N_SPLIT
QK_DEPTH = 2 * LANES

TOKEN_TILE = 512
FF_CHUNK = 256
ATTN_TILE = 256
VMEM_LIMIT = 56 * 1024 * 1024
MASK_VALUE = -1e30

bf16 = jnp.bfloat16
f32 = jnp.float32


def _rms_scale(x):
    return lax.rsqrt(jnp.mean(x * x, axis=-1, keepdims=True) + EPS)


def _swiglu_residual(x, norm_w, wg_ref, wu_ref, wd_ref):
    h = (x * _rms_scale(x) * norm_w).astype(bf16)
    acc = jnp.zeros(x.shape, f32)
    for c in range(D_FF // FF_CHUNK):
        cols = slice(c * FF_CHUNK, (c + 1) * FF_CHUNK)
        g = jnp.dot(h, wg_ref[:, cols], preferred_element_type=f32)
        u = jnp.dot(h, wu_ref[:, cols], preferred_element_type=f32)
        a = (g * jax.nn.sigmoid(g) * u).astype(bf16)
        acc = acc + jnp.dot(a, wd_ref[cols, :], preferred_element_type=f32)
    return x + 0.5 * acc


def _ffn1_kernel(x_ref, nw_ref, wg_ref, wu_ref, wd_ref, o_ref):
    o_ref[...] = _swiglu_residual(x_ref[...], nw_ref[...], wg_ref, wu_ref, wd_ref)


def _mix_ffn2_kernel(x_ref, yp_ref, oa_ref, nattn_ref, wo_ref, nw_ref, wg_ref, wu_ref, wd_ref, o_ref):
    oa = oa_ref[...]
    ya = (oa * _rms_scale(oa) * nattn_ref[...]).astype(bf16)
    x2 = (x_ref[...]
          + jnp.dot(yp_ref[...], wo_ref[:POOL_WIDTH, :], preferred_element_type=f32)
          + jnp.dot(ya, wo_ref[POOL_WIDTH:, :], preferred_element_type=f32))
    o_ref[...] = _swiglu_residual(x2, nw_ref[...], wg_ref, wu_ref, wd_ref)


def _split3(x):
    hi = x.astype(bf16).astype(f32)
    r = x - hi
    mid = r.astype(bf16).astype(f32)
    lo = (r - mid).astype(bf16).astype(f32)
    return hi, mid, lo


def _in_proj_kernel(x_ref, nw_ref, wpv_ref, wk_ref, wqvt_ref, vones_ref, wf_ref, bf_ref,
                    poolw_ref, pscale_ref, npool_ref, qn_ref, kn_ref, headsum_ref, tri_ref,
                    selk_ref, ck_ref,
                    yp_ref, qt_ref, qaugt_ref, ka_ref, vt_ref,
                    halo_ref, carry_ref):
    s_idx = pl.program_id(1)
    tm = x_ref.shape[1]

    @pl.when(s_idx == 0)
    def _():
        halo_ref[...] = jnp.zeros(halo_ref.shape, f32)
        carry_ref[...] = jnp.zeros(carry_ref.shape, f32)

    x = x_ref[0]
    h = (x * _rms_scale(x) * nw_ref[...]).astype(bf16)

    t = jnp.dot(h, wk_ref[...], preferred_element_type=f32)
    ssq = jnp.dot((t * t).astype(bf16), headsum_ref[...], preferred_element_type=f32)
    kn = (t * lax.rsqrt(ssq * (1.0 / HEAD_DIM) + EPS) * kn_ref[...]).astype(bf16)

    qvt = lax.dot_general(wqvt_ref[...], h, (((1,), (1,)), ((), ())), preferred_element_type=f32)
    qt = qvt[:ATTN_WIDTH].reshape(N_HEADS, HEAD_DIM, tm)
    qt = qt * lax.rsqrt(jnp.mean(qt * qt, axis=1, keepdims=True) + EPS)
    qt_ref[0] = (qt.reshape(ATTN_WIDTH, tm) * qn_ref[...]).astype(bf16)
    vt = (qvt[ATTN_WIDTH:] + vones_ref[...]).astype(bf16)
    vt_ref[0] = vt.reshape(N_HEADS, LANES, tm)

    fl = jnp.dot(h, wf_ref[...], preferred_element_type=f32) + bf_ref[...]
    logf = (jnp.minimum(fl, 0.0) - jnp.log1p(jnp.exp(-jnp.abs(fl)))) * LOG2E
    lane = lax.broadcasted_iota(jnp.int32, logf.shape, 1)
    logf = jnp.where(lane < N_HEADS, logf, 0.0)
    tri = tri_ref[...]
    hi, mid, lo = _split3(logf)
    fcum = (jnp.dot(tri, hi.astype(bf16), preferred_element_type=f32)
            + jnp.dot(tri, mid.astype(bf16), preferred_element_type=f32)
            + jnp.dot(tri, lo.astype(bf16), preferred_element_type=f32)
            + carry_ref[...])
    carry_ref[...] = fcum[tm - 1:tm, :]

    hi, mid, lo = _split3(fcum)
    packed = (hi + pltpu.roll(mid, N_HEADS, 1) + pltpu.roll(lo, 2 * N_HEADS, 1)).astype(bf16)
    kaug = (jnp.dot(packed, selk_ref[...], preferred_element_type=f32) + ck_ref[...]).astype(bf16)
    pieces = []
    for p in range(N_HEAD_BLOCKS):
        cols = slice(p * LANES, (p + 1) * LANES)
        pieces += [kn[:, cols], kaug[:, cols]]
    ka_ref[0] = jnp.concatenate(pieces, axis=1)

    hi_t, mid_t, lo_t = _split3(fcum.T[:N_HEADS])
    rowid = lax.broadcasted_iota(jnp.int32, (BF16_ROWS, tm), 0)
    for hh in range(N_HEADS):
        base = (hh % HEADS_PER_BLOCK) * AUG_STRIDE
        ones_rows = ((rowid >= base + N_SPLIT) & (rowid < base + 2 * N_SPLIT)).astype(f32)
        aug = jnp.where(rowid == base, hi_t[hh:hh + 1],
                        jnp.where(rowid == base + 1, mid_t[hh:hh + 1],
                                  jnp.where(rowid == base + 2, lo_t[hh:hh + 1], ones_rows)))
        qaugt_ref[0, hh] = aug.astype(bf16)

    pv = jnp.dot(h, wpv_ref[...], preferred_element_type=f32)
    full = jnp.concatenate([halo_ref[...], pv], axis=0)
    halo_ref[...] = pv[tm - POOL_HALO:, :]
    pos = s_idx * tm + lax.broadcasted_iota(jnp.int32, (tm, 1), 0)
    mixed = []
    for g in range(POOL_GROUPS):
        w = POOL_WINDOWS[g]
        cols = slice(g * POOL_GROUP_DIM, (g + 1) * POOL_GROUP_DIM)
        win = full[:, cols]
        shift = 1
        while shift < w:
            win = win + pltpu.roll(win, shift, 0)
            shift *= 2
        count = jnp.minimum(pos + 1, w).astype(f32)
        pooled = win[POOL_HALO:, :] / count - pv[:, cols]
        mixed.append(jnp.dot(pooled.astype(bf16), poolw_ref[g], preferred_element_type=f32))
    mixed = jnp.concatenate(mixed, axis=1) * pscale_ref[...]
    yp_ref[0] = (mixed * _rms_scale(mixed) * npool_ref[...]).astype(bf16)


def _attn_kernel(qt_ref, qaugt_ref, ka_ref, vt_ref, o_ref, qat_ref, s_ref, m_ref, acc_ref):
    i = pl.program_id(1)
    tq = o_ref.shape[1]
    tk = tq

    zeros_head = jnp.zeros((HEAD_DIM, tq), bf16)
    for hh in range(N_HEADS):
        qh = qt_ref[0, hh * HEAD_DIM:(hh + 1) * HEAD_DIM, :]
        odd = hh % HEADS_PER_BLOCK
        qat_ref[hh, 0:HEAD_DIM, :] = zeros_head if odd else qh
        qat_ref[hh, HEAD_DIM:LANES, :] = qh if odd else zeros_head
        qat_ref[hh, LANES:LANES + BF16_ROWS, :] = qaugt_ref[0, hh]
        qat_ref[hh, LANES + BF16_ROWS:, :] = jnp.zeros((QK_DEPTH - LANES - BF16_ROWS, tq), bf16)

    m_ref[...] = jnp.full(m_ref.shape, MASK_VALUE, f32)
    acc_ref[...] = jnp.zeros(acc_ref.shape, f32)
    key_minus_query = (lax.broadcasted_iota(jnp.int32, (tk, tq), 0)
                       - lax.broadcasted_iota(jnp.int32, (tk, tq), 1))

    def block(j, masked):
        start = pl.multiple_of(j * tk, tk)

        def logits(hh):
            p = hh // HEADS_PER_BLOCK
            ka = ka_ref[0, pl.ds(start, tk), p * QK_DEPTH:(p + 1) * QK_DEPTH]
            st = jnp.dot(ka, qat_ref[hh], preferred_element_type=f32)
            if masked:
                st = jnp.where(key_minus_query > 0, MASK_VALUE, st)
            s_ref[hh] = st

        def softmax(hh):
            m_prev = m_ref[hh]
            m_new = jnp.maximum(m_prev, jnp.max(s_ref[hh], axis=0, keepdims=True))
            m_ref[hh] = m_new
            return jnp.exp2(m_prev - m_new), jnp.exp2(s_ref[hh] - m_new).astype(bf16)

        def values(hh, alpha, pt):
            vt = vt_ref[0, hh, :, pl.ds(start, tk)]
            acc_ref[hh] = alpha * acc_ref[hh] + jnp.dot(vt, pt, preferred_element_type=f32)

        for hh in range(N_HEADS):
            logits(hh)
        for hh in range(N_HEADS):
            values(hh, *softmax(hh))

    def body(j, carry):
        block(j, masked=False)
        return carry

    lax.fori_loop(0, i, body, 0)
    block(i, masked=True)

    row = lax.broadcasted_iota(jnp.int32, (LANES, tq), 0)
    for p in range(N_HEAD_BLOCKS):
        even, odd = acc_ref[2 * p], acc_ref[2 * p + 1]
        ot = jnp.where(row < HEAD_DIM, even / even[HEAD_DIM:HEAD_DIM + 1, :], odd / odd[0:1, :])
        o_ref[0, :, p * LANES:(p + 1) * LANES] = ot.T


def _resident(shape):
    nd = len(shape)
    return pl.BlockSpec(shape, lambda *_: (0,) * nd, pipeline_mode=pl.Buffered(1))


def _ffn_weight_specs():
    return [_resident((1, D_MODEL)), _resident((D_MODEL, D_FF)), _resident((D_MODEL, D_FF)),
            _resident((D_FF, D_MODEL))]


def _k_aug_constants():
    selk = np.zeros((LANES, N_HEAD_BLOCKS * LANES), np.float32)
    ck = np.zeros((1, N_HEAD_BLOCKS * LANES), np.float32)
    for hh in range(N_HEADS):
        p, odd = divmod(hh, HEADS_PER_BLOCK)
        base = odd * AUG_STRIDE
        for part in range(N_SPLIT):
            ck[0, p * LANES + base + part] = 1.0
            selk[part * N_HEADS + hh, p * LANES + base + N_SPLIT + part] = -1.0
    return jnp.asarray(selk, bf16), jnp.asarray(ck)


def _qvt_weights(w_q, w_v):
    wvt = jnp.zeros((N_HEADS, LANES, D_MODEL), w_v.dtype)
    ones = np.zeros((N_HEADS, LANES, 1), np.float32)
    for hh in range(N_HEADS):
        odd = hh % HEADS_PER_BLOCK
        lo = HEAD_DIM if odd else 0
        wvt = wvt.at[hh, lo:lo + HEAD_DIM, :].set(w_v[:, hh * HEAD_DIM:(hh + 1) * HEAD_DIM].T)
        ones[hh, 0 if odd else HEAD_DIM, 0] = 1.0
    wqvt = jnp.concatenate([w_q.T, wvt.reshape(N_HEADS * LANES, D_MODEL)], axis=0).astype(bf16)
    return wqvt, jnp.asarray(ones.reshape(N_HEADS * LANES, 1))


def kernel(x, ffn1_norm, ffn1_w_gate, ffn1_w_up, ffn1_w_down, mix_norm, w_in, b_forget, pool_w, pool_scale, q_norm, k_norm, out_norm_pool, out_norm_attn, w_out, ffn2_norm, ffn2_w_gate, ffn2_w_up, ffn2_w_down):
    B, S, D = x.shape
    assert D == D_MODEL and S % TOKEN_TILE == 0 and S % ATTN_TILE == 0
    T = B * S
    n_tok = T // TOKEN_TILE
    row = lambda a: a.reshape(1, -1).astype(f32)

    tok_spec = pl.BlockSpec((TOKEN_TILE, D_MODEL), lambda t: (t, 0))
    params_1d = pltpu.CompilerParams(dimension_semantics=("arbitrary",), vmem_limit_bytes=VMEM_LIMIT)
    params_2d = pltpu.CompilerParams(dimension_semantics=("arbitrary", "arbitrary"), vmem_limit_bytes=VMEM_LIMIT)

    x1 = pl.pallas_call(
        _ffn1_kernel,
        grid=(n_tok,),
        in_specs=[tok_spec] + _ffn_weight_specs(),
        out_specs=tok_spec,
        out_shape=jax.ShapeDtypeStruct((T, D_MODEL), f32),
        compiler_params=params_1d,
        name="ffn1",
    )(x.reshape(T, D_MODEL), row(ffn1_norm), ffn1_w_gate.astype(bf16), ffn1_w_up.astype(bf16),
      ffn1_w_down.astype(bf16))

    c0 = POOL_WIDTH
    w_pv = w_in[:, :c0].astype(bf16)
    w_k = w_in[:, c0 + ATTN_WIDTH:c0 + 2 * ATTN_WIDTH].astype(bf16)
    w_qvt, v_ones = _qvt_weights(w_in[:, c0:c0 + ATTN_WIDTH], w_in[:, c0 + 2 * ATTN_WIDTH:c0 + 3 * ATTN_WIDTH])
    w_f = jnp.pad(w_in[:, c0 + 3 * ATTN_WIDTH:], ((0, 0), (0, LANES - N_HEADS))).astype(bf16)
    b_f = jnp.pad(b_forget.astype(f32), (0, LANES - N_HEADS)).reshape(1, LANES)
    q_gain = (jnp.tile(q_norm, N_HEADS).astype(f32) * (LOG2E / math.sqrt(HEAD_DIM))).reshape(ATTN_WIDTH, 1)
    k_gain = row(jnp.tile(k_norm, N_HEADS))
    head_id = jnp.arange(ATTN_WIDTH) // HEAD_DIM
    headsum = (head_id[:, None] == head_id[None, :]).astype(bf16)
    tri = (jnp.arange(TOKEN_TILE)[:, None] >= jnp.arange(TOKEN_TILE)[None, :]).astype(bf16)
    selk, ck = _k_aug_constants()

    seq_spec = lambda width: pl.BlockSpec((1, TOKEN_TILE, width), lambda b, s: (b, s, 0))
    n_qvt = ATTN_WIDTH + N_HEADS * LANES
    y_pool, qt, qaugt, ka, vt = pl.pallas_call(
        _in_proj_kernel,
        grid=(B, S // TOKEN_TILE),
        in_specs=[seq_spec(D_MODEL), _resident((1, D_MODEL)),
                  _resident((D_MODEL, POOL_WIDTH)), _resident((D_MODEL, ATTN_WIDTH)),
                  _resident((n_qvt, D_MODEL)), _resident((N_HEADS * LANES, 1)),
                  _resident((D_MODEL, LANES)), _resident((1, LANES)),
                  _resident((POOL_GROUPS, POOL_GROUP_DIM, POOL_GROUP_DIM)), _resident((1, POOL_WIDTH)),
                  _resident((1, POOL_WIDTH)), _resident((ATTN_WIDTH, 1)), _resident((1, ATTN_WIDTH)),
                  _resident((ATTN_WIDTH, ATTN_WIDTH)), _resident((TOKEN_TILE, TOKEN_TILE)),
                  _resident((LANES, N_HEAD_BLOCKS * LANES)), _resident((1, N_HEAD_BLOCKS * LANES))],
        out_specs=[seq_spec(POOL_WIDTH),
                   pl.BlockSpec((1, ATTN_WIDTH, TOKEN_TILE), lambda b, s: (b, 0, s)),
                   pl.BlockSpec((1, N_HEADS, BF16_ROWS, TOKEN_TILE), lambda b, s: (b, 0, 0, s)),
                   seq_spec(N_HEAD_BLOCKS * QK_DEPTH),
                   pl.BlockSpec((1, N_HEADS, LANES, TOKEN_TILE), lambda b, s: (b, 0, 0, s))],
        out_shape=[jax.ShapeDtypeStruct((B, S, POOL_WIDTH), bf16),
                   jax.ShapeDtypeStruct((B, ATTN_WIDTH, S), bf16),
                   jax.ShapeDtypeStruct((B, N_HEADS, BF16_ROWS, S), bf16),
                   jax.ShapeDtypeStruct((B, S, N_HEAD_BLOCKS * QK_DEPTH), bf16),
                   jax.ShapeDtypeStruct((B, N_HEADS, LANES, S), bf16)],
        scratch_shapes=[pltpu.VMEM((POOL_HALO, POOL_WIDTH), f32), pltpu.VMEM((1, LANES), f32)],
        compiler_params=params_2d,
        name="in_proj",
    )(x1.reshape(B, S, D_MODEL), row(mix_norm), w_pv, w_k, w_qvt, v_ones, w_f, b_f,
      pool_w.astype(bf16), row(pool_scale), row(out_norm_pool), q_gain, k_gain, headsum, tri, selk, ck)

    o_attn = pl.pallas_call(
        _attn_kernel,
        grid=(B, S // ATTN_TILE),
        in_specs=[pl.BlockSpec((1, ATTN_WIDTH, ATTN_TILE), lambda b, i: (b, 0, i)),
                  pl.BlockSpec((1, N_HEADS, BF16_ROWS, ATTN_TILE), lambda b, i: (b, 0, 0, i)),
                  pl.BlockSpec((1, S, N_HEAD_BLOCKS * QK_DEPTH), lambda b, i: (b, 0, 0)),
                  pl.BlockSpec((1, N_HEADS, LANES, S), lambda b, i: (b, 0, 0, 0))],
        out_specs=pl.BlockSpec((1, ATTN_TILE, ATTN_WIDTH), lambda b, i: (b, i, 0)),
        out_shape=jax.ShapeDtypeStruct((B, S, ATTN_WIDTH), f32),
        scratch_shapes=[pltpu.VMEM((N_HEADS, QK_DEPTH, ATTN_TILE), bf16),
                        pltpu.VMEM((N_HEADS, ATTN_TILE, ATTN_TILE), f32),
                        pltpu.VMEM((N_HEADS, 1, ATTN_TILE), f32),
                        pltpu.VMEM((N_HEADS, LANES, ATTN_TILE), f32)],
        compiler_params=params_2d,
        name="attention",
    )(qt, qaugt, ka, vt)

    half_spec = pl.BlockSpec((TOKEN_TILE, POOL_WIDTH), lambda t: (t, 0))
    out = pl.pallas_call(
        _mix_ffn2_kernel,
        grid=(n_tok,),
        in_specs=[tok_spec, half_spec, half_spec, _resident((1, ATTN_WIDTH)),
                  _resident((D_MODEL, D_MODEL))] + _ffn_weight_specs(),
        out_specs=tok_spec,
        out_shape=jax.ShapeDtypeStruct((T, D_MODEL), f32),
        compiler_params=params_1d,
        name="mix_ffn2",
    )(x1, y_pool.reshape(T, POOL_WIDTH), o_attn.reshape(T, ATTN_WIDTH), row(out_norm_attn),
      w_out.astype(bf16), row(ffn2_norm), ffn2_w_gate.astype(bf16), ffn2_w_up.astype(bf16),
      ffn2_w_down.astype(bf16))
    return out.reshape(B, S, D_MODEL)
```

```python
import math

import numpy as np
import jax
import jax.numpy as jnp
from jax import lax
from jax.experimental import pallas as pl
from jax.experimental.pallas import tpu as pltpu

D_MODEL = 1024
D_FF = 2816
POOL_WIDTH = 512
POOL_GROUPS = 4
POOL_GROUP_DIM = 128
POOL_WINDOWS = (2, 4, 8, 16)
ATTN_WIDTH = 512
HEAD_DIM = 64
N_HEADS = 8
EPS = 1e-6
LOG2E = math.log2(math.e)

LANES = 128
BF16_ROWS = 16
HEADS_PER_BLOCK = LANES // HEAD_DIM
N_HEAD_BLOCKS = N_HEADS // HEADS_PER_BLOCK
POOL_HALO = 16
N_SPLIT = 3
AUG_STRIDE = 2 * N_SPLIT
QK_DEPTH = 2 * LANES

TOKEN_TILE = 512
FF_CHUNK = 256
ATTN_TILE = 256
VMEM_LIMIT = 56 * 1024 * 1024
MASK_VALUE = -1e30

bf16 = jnp.bfloat16
f32 = jnp.float32


def _rms_scale(x):
    return lax.rsqrt(jnp.mean(x * x, axis=-1, keepdims=True) + EPS)


def _swiglu_residual(x, norm_w, wg_ref, wu_ref, wd_ref):
    h = (x * _rms_scale(x) * norm_w).astype(bf16)
    acc = jnp.zeros(x.shape, f32)
    for c in range(D_FF // FF_CHUNK):
        cols = slice(c * FF_CHUNK, (c + 1) * FF_CHUNK)
        g = jnp.dot(h, wg_ref[:, cols], preferred_element_type=f32)
        u = jnp.dot(h, wu_ref[:, cols], preferred_element_type=f32)
        a = (g * jax.nn.sigmoid(g) * u).astype(bf16)
        acc = acc + jnp.dot(a, wd_ref[cols, :], preferred_element_type=f32)
    return x + 0.5 * acc


def _ffn1_kernel(x_ref, nw_ref, wg_ref, wu_ref, wd_ref, o_ref):
    o_ref[...] = _swiglu_residual(x_ref[...], nw_ref[...], wg_ref, wu_ref, wd_ref)


def _mix_ffn2_kernel(x_ref, yp_ref, oa_ref, nattn_ref, wo_ref, nw_ref, wg_ref, wu_ref, wd_ref, o_ref):
    oa = oa_ref[...]
    ya = (oa * _rms_scale(oa) * nattn_ref[...]).astype(bf16)
    x2 = (x_ref[...]
          + jnp.dot(yp_ref[...], wo_ref[:POOL_WIDTH, :], preferred_element_type=f32)
          + jnp.dot(ya, wo_ref[POOL_WIDTH:, :], preferred_element_type=f32))
    o_ref[...] = _swiglu_residual(x2, nw_ref[...], wg_ref, wu_ref, wd_ref)


def _split3(x):
    hi = x.astype(bf16).astype(f32)
    r = x - hi
    mid = r.astype(bf16).astype(f32)
    lo = (r - mid).astype(bf16).astype(f32)
    return hi, mid, lo


def _in_proj_kernel(x_ref, nw_ref, wpv_ref, wk_ref, wqvt_ref, vones_ref, wf_ref, bf_ref,
                    poolw_ref, pscale_ref, npool_ref, qn_ref, kn_ref, headsum_ref, tri_ref,
                    selk_ref, ck_ref,
                    yp_ref, qt_ref, qaugt_ref, ka_ref, vt_ref,
                    halo_ref, carry_ref):
    s_idx = pl.program_id(1)
    tm = x_ref.shape[1]

    @pl.when(s_idx == 0)
    def _():
        halo_ref[...] = jnp.zeros(halo_ref.shape, f32)
        carry_ref[...] = jnp.zeros(carry_ref.shape, f32)

    x = x_ref[0]
    h = (x * _rms_scale(x) * nw_ref[...]).astype(bf16)

    t = jnp.dot(h, wk_ref[...], preferred_element_type=f32)
    ssq = jnp.dot((t * t).astype(bf16), headsum_ref[...], preferred_element_type=f32)
    kn = (t * lax.rsqrt(ssq * (1.0 / HEAD_DIM) + EPS) * kn_ref[...]).astype(bf16)

    qvt = lax.dot_general(wqvt_ref[...], h, (((1,), (1,)), ((), ())), preferred_element_type=f32)
    qt = qvt[:ATTN_WIDTH].reshape(N_HEADS, HEAD_DIM, tm)
    qt = qt * lax.rsqrt(jnp.mean(qt * qt, axis=1, keepdims=True) + EPS)
    qt_ref[0] = (qt.reshape(ATTN_WIDTH, tm) * qn_ref[...]).astype(bf16)
    vt = (qvt[ATTN_WIDTH:] + vones_ref[...]).astype(bf16)
    vt_ref[0] = vt.reshape(N_HEADS, LANES, tm)

    fl = jnp.dot(h, wf_ref[...], preferred_element_type=f32) + bf_ref[...]
    logf = (jnp.minimum(fl, 0.0) - jnp.log1p(jnp.exp(-jnp.abs(fl)))) * LOG2E
    lane = lax.broadcasted_iota(jnp.int32, logf.shape, 1)
    logf = jnp.where(lane < N_HEADS, logf, 0.0)
    tri = tri_ref[...]
    hi, mid, lo = _split3(logf)
    fcum = (jnp.dot(tri, hi.astype(bf16), preferred_element_type=f32)
            + jnp.dot(tri, mid.astype(bf16), preferred_element_type=f32)
            + jnp.dot(tri, lo.astype(bf16), preferred_element_type=f32)
            + carry_ref[...])
    carry_ref[...] = fcum[tm - 1:tm, :]

    hi, mid, lo = _split3(fcum)
    packed = (hi + pltpu.roll(mid, N_HEADS, 1) + pltpu.roll(lo, 2 * N_HEADS, 1)).astype(bf16)
    kaug = (jnp.dot(packed, selk_ref[...], preferred_element_type=f32) + ck_ref[...]).astype(bf16)
    pieces = []
    for p in range(N_HEAD_BLOCKS):
        cols = slice(p * LANES, (p + 1) * LANES)
        pieces += [kn[:, cols], kaug[:, cols]]
    ka_ref[0] = jnp.concatenate(pieces, axis=1)

    hi_t, mid_t, lo_t = _split3(fcum.T[:N_HEADS])
    rowid = lax.broadcasted_iota(jnp.int32, (BF16_ROWS, tm), 0)
    for hh in range(N_HEADS):
        base = (hh % HEADS_PER_BLOCK) * AUG_STRIDE
        ones_rows = ((rowid >= base + N_SPLIT) & (rowid < base + 2 * N_SPLIT)).astype(f32)
        aug = jnp.where(rowid == base, hi_t[hh:hh + 1],
                        jnp.where(rowid == base + 1, mid_t[hh:hh + 1],
                                  jnp.where(rowid == base + 2, lo_t[hh:hh + 1], ones_rows)))
        qaugt_ref[0, hh] = aug.astype(bf16)

    pv = jnp.dot(h, wpv_ref[...], preferred_element_type=f32)
    full = jnp.concatenate([halo_ref[...], pv], axis=0)
    halo_ref[...] = pv[tm - POOL_HALO:, :]
    pos = s_idx * tm + lax.broadcasted_iota(jnp.int32, (tm, 1), 0)
    mixed = []
    for g in range(POOL_GROUPS):
        w = POOL_WINDOWS[g]
        cols = slice(g * POOL_GROUP_DIM, (g + 1) * POOL_GROUP_DIM)
        win = full[:, cols]
        shift = 1
        while shift < w:
            win = win + pltpu.roll(win, shift, 0)
            shift *= 2
        count = jnp.minimum(pos + 1, w).astype(f32)
        pooled = win[POOL_HALO:, :] / count - pv[:, cols]
        mixed.append(jnp.dot(pooled.astype(bf16), poolw_ref[g], preferred_element_type=f32))
    mixed = jnp.concatenate(mixed, axis=1) * pscale_ref[...]
    yp_ref[0] = (mixed * _rms_scale(mixed) * npool_ref[...]).astype(bf16)


def _attn_kernel(qt_ref, qaugt_ref, ka_ref, vt_ref, o_ref, qat_ref, s_ref, m_ref, acc_ref):
    i = pl.program_id(1)
    tq = o_ref.shape[1]
    tk = tq

    zeros_head = jnp.zeros((HEAD_DIM, tq), bf16)
    for hh in range(N_HEADS):
        qh = qt_ref[0, hh * HEAD_DIM:(hh + 1) * HEAD_DIM, :]
        odd = hh % HEADS_PER_BLOCK
        qat_ref[hh, 0:HEAD_DIM, :] = zeros_head if odd else qh
        qat_ref[hh, HEAD_DIM:LANES, :] = qh if odd else zeros_head
        qat_ref[hh, LANES:LANES + BF16_ROWS, :] = qaugt_ref[0, hh]
        qat_ref[hh, LANES + BF16_ROWS:, :] = jnp.zeros((QK_DEPTH - LANES - BF16_ROWS, tq), bf16)

    m_ref[...] = jnp.full(m_ref.shape, MASK_VALUE, f32)
    acc_ref[...] = jnp.zeros(acc_ref.shape, f32)
    key_minus_query = (lax.broadcasted_iota(jnp.int32, (tk, tq), 0)
                       - lax.broadcasted_iota(jnp.int32, (tk, tq), 1))

    def block(j, masked):
        keys = slice(j * tk, (j + 1) * tk)
        slot = j % 2

        def logits(hh):
            p = hh // HEADS_PER_BLOCK
            ka = ka_ref[0, keys, p * QK_DEPTH:(p + 1) * QK_DEPTH]
            st = jnp.dot(ka, qat_ref[hh], preferred_element_type=f32)
            if masked:
                st = jnp.where(key_minus_query > 0, MASK_VALUE, st)
            s_ref[slot, hh] = st

        def softmax(hh):
            m_prev = m_ref[hh]
            m_new = jnp.maximum(m_prev, jnp.max(s_ref[slot, hh], axis=0, keepdims=True))
            m_ref[hh] = m_new
            return jnp.exp2(m_prev - m_new), jnp.exp2(s_ref[slot, hh] - m_new).astype(bf16)

        def values(hh, alpha, pt):
            vt = vt_ref[0, hh, :, keys]
            acc_ref[hh] = alpha * acc_ref[hh] + jnp.dot(vt, pt, preferred_element_type=f32)

        for hh in range(N_HEADS):
            logits(hh)
        for hh in range(N_HEADS):
            values(hh, *softmax(hh))

    for i_static in range(ka_ref.shape[1] // tk):
        @pl.when(i == i_static)
        def _(n_blocks=i_static + 1):
            for j in range(n_blocks):
                block(j, masked=(j == n_blocks - 1))

    row = lax.broadcasted_iota(jnp.int32, (LANES, tq), 0)
    for p in range(N_HEAD_BLOCKS):
        even, odd = acc_ref[2 * p], acc_ref[2 * p + 1]
        ot = jnp.where(row < HEAD_DIM, even / even[HEAD_DIM:HEAD_DIM + 1, :], odd / odd[0:1, :])
        o_ref[0, :, p * LANES:(p + 1) * LANES] = ot.T


def _resident(shape):
    nd = len(shape)
    return pl.BlockSpec(shape, lambda *_: (0,) * nd, pipeline_mode=pl.Buffered(1))


def _ffn_weight_specs():
    return [_resident((1, D_MODEL)), _resident((D_MODEL, D_FF)), _resident((D_MODEL, D_FF)),
            _resident((D_FF, D_MODEL))]


def _k_aug_constants():
    selk = np.zeros((LANES, N_HEAD_BLOCKS * LANES), np.float32)
    ck = np.zeros((1, N_HEAD_BLOCKS * LANES), np.float32)
    for hh in range(N_HEADS):
        p, odd = divmod(hh, HEADS_PER_BLOCK)
        base = odd * AUG_STRIDE
        for part in range(N_SPLIT):
            ck[0, p * LANES + base + part] = 1.0
            selk[part * N_HEADS + hh, p * LANES + base + N_SPLIT + part] = -1.0
    return jnp.asarray(selk, bf16), jnp.asarray(ck)


def _qvt_weights(w_q, w_v):
    wvt = jnp.zeros((N_HEADS, LANES, D_MODEL), w_v.dtype)
    ones = np.zeros((N_HEADS, LANES, 1), np.float32)
    for hh in range(N_HEADS):
        odd = hh % HEADS_PER_BLOCK
        lo = HEAD_DIM if odd else 0
        wvt = wvt.at[hh, lo:lo + HEAD_DIM, :].set(w_v[:, hh * HEAD_DIM:(hh + 1) * HEAD_DIM].T)
        ones[hh, 0 if odd else HEAD_DIM, 0] = 1.0
    wqvt = jnp.concatenate([w_q.T, wvt.reshape(N_HEADS * LANES, D_MODEL)], axis=0).astype(bf16)
    return wqvt, jnp.asarray(ones.reshape(N_HEADS * LANES, 1))


def kernel(x, ffn1_norm, ffn1_w_gate, ffn1_w_up, ffn1_w_down, mix_norm, w_in, b_forget, pool_w, pool_scale, q_norm, k_norm, out_norm_pool, out_norm_attn, w_out, ffn2_norm, ffn2_w_gate, ffn2_w_up, ffn2_w_down):
    B, S, D = x.shape
    assert D == D_MODEL and S % TOKEN_TILE == 0 and S % ATTN_TILE == 0
    T = B * S
    n_tok = T // TOKEN_TILE
    row = lambda a: a.reshape(1, -1).astype(f32)

    tok_spec = pl.BlockSpec((TOKEN_TILE, D_MODEL), lambda t: (t, 0))
    params_1d = pltpu.CompilerParams(dimension_semantics=("arbitrary",), vmem_limit_bytes=VMEM_LIMIT)
    params_2d = pltpu.CompilerParams(dimension_semantics=("arbitrary", "arbitrary"), vmem_limit_bytes=VMEM_LIMIT)

    x1 = pl.pallas_call(
        _ffn1_kernel,
        grid=(n_tok,),
        in_specs=[tok_spec] + _ffn_weight_specs(),
        out_specs=tok_spec,
        out_shape=jax.ShapeDtypeStruct((T, D_MODEL), f32),
        compiler_params=params_1d,
        name="ffn1",
    )(x.reshape(T, D_MODEL), row(ffn1_norm), ffn1_w_gate.astype(bf16), ffn1_w_up.astype(bf16),
      ffn1_w_down.astype(bf16))

    c0 = POOL_WIDTH
    w_pv = w_in[:, :c0].astype(bf16)
    w_k = w_in[:, c0 + ATTN_WIDTH:c0 + 2 * ATTN_WIDTH].astype(bf16)
    w_qvt, v_ones = _qvt_weights(w_in[:, c0:c0 + ATTN_WIDTH], w_in[:, c0 + 2 * ATTN_WIDTH:c0 + 3 * ATTN_WIDTH])
    w_f = jnp.pad(w_in[:, c0 + 3 * ATTN_WIDTH:], ((0, 0), (0, LANES - N_HEADS))).astype(bf16)
    b_f = jnp.pad(b_forget.astype(f32), (0, LANES - N_HEADS)).reshape(1, LANES)
    q_gain = (jnp.tile(q_norm, N_HEADS).astype(f32) * (LOG2E / math.sqrt(HEAD_DIM))).reshape(ATTN_WIDTH, 1)
    k_gain = row(jnp.tile(k_norm, N_HEADS))
    head_id = jnp.arange(ATTN_WIDTH) // HEAD_DIM
    headsum = (head_id[:, None] == head_id[None, :]).astype(bf16)
    tri = (jnp.arange(TOKEN_TILE)[:, None] >= jnp.arange(TOKEN_TILE)[None, :]).astype(bf16)
    selk, ck = _k_aug_constants()

    seq_spec = lambda width: pl.BlockSpec((1, TOKEN_TILE, width), lambda b, s: (b, s, 0))
    n_qvt = ATTN_WIDTH + N_HEADS * LANES
    y_pool, qt, qaugt, ka, vt = pl.pallas_call(
        _in_proj_kernel,
        grid=(B, S // TOKEN_TILE),
        in_specs=[seq_spec(D_MODEL), _resident((1, D_MODEL)),
                  _resident((D_MODEL, POOL_WIDTH)), _resident((D_MODEL, ATTN_WIDTH)),
                  _resident((n_qvt, D_MODEL)), _resident((N_HEADS * LANES, 1)),
                  _resident((D_MODEL, LANES)), _resident((1, LANES)),
                  _resident((POOL_GROUPS, POOL_GROUP_DIM, POOL_GROUP_DIM)), _resident((1, POOL_WIDTH)),
                  _resident((1, POOL_WIDTH)), _resident((ATTN_WIDTH, 1)), _resident((1, ATTN_WIDTH)),
                  _resident((ATTN_WIDTH, ATTN_WIDTH)), _resident((TOKEN_TILE, TOKEN_TILE)),
                  _resident((LANES, N_HEAD_BLOCKS * LANES)), _resident((1, N_HEAD_BLOCKS * LANES))],
        out_specs=[seq_spec(POOL_WIDTH),
                   pl.BlockSpec((1, ATTN_WIDTH, TOKEN_TILE), lambda b, s: (b, 0, s)),
                   pl.BlockSpec((1, N_HEADS, BF16_ROWS, TOKEN_TILE), lambda b, s: (b, 0, 0, s)),
                   seq_spec(N_HEAD_BLOCKS * QK_DEPTH),
                   pl.BlockSpec((1, N_HEADS, LANES, TOKEN_TILE), lambda b, s: (b, 0, 0, s))],
        out_shape=[jax.ShapeDtypeStruct((B, S, POOL_WIDTH), bf16),
                   jax.ShapeDtypeStruct((B, ATTN_WIDTH, S), bf16),
                   jax.ShapeDtypeStruct((B, N_HEADS, BF16_ROWS, S), bf16),
                   jax.ShapeDtypeStruct((B, S, N_HEAD_BLOCKS * QK_DEPTH), bf16),
                   jax.ShapeDtypeStruct((B, N_HEADS, LANES, S), bf16)],
        scratch_shapes=[pltpu.VMEM((POOL_HALO, POOL_WIDTH), f32), pltpu.VMEM((1, LANES), f32)],
        compiler_params=params_2d,
        name="in_proj",
    )(x1.reshape(B, S, D_MODEL), row(mix_norm), w_pv, w_k, w_qvt, v_ones, w_f, b_f,
      pool_w.astype(bf16), row(pool_scale), row(out_norm_pool), q_gain, k_gain, headsum, tri, selk, ck)

    o_attn = pl.pallas_call(
        _attn_kernel,
        grid=(B, S // ATTN_TILE),
        in_specs=[pl.BlockSpec((1, ATTN_WIDTH, ATTN_TILE), lambda b, i: (b, 0, i)),
                  pl.BlockSpec((1, N_HEADS, BF16_ROWS, ATTN_TILE), lambda b, i: (b, 0, 0, i)),
                  pl.BlockSpec((1, S, N_HEAD_BLOCKS * QK_DEPTH), lambda b, i: (b, 0, 0)),
                  pl.BlockSpec((1, N_HEADS, LANES, S), lambda b, i: (b, 0, 0, 0))],
        out_specs=pl.BlockSpec((1, ATTN_TILE, ATTN_WIDTH), lambda b, i: (b, i, 0)),
        out_shape=jax.ShapeDtypeStruct((B, S, ATTN_WIDTH), f32),
        scratch_shapes=[pltpu.VMEM((N_HEADS, QK_DEPTH, ATTN_TILE), bf16),
                        pltpu.VMEM((2, N_HEADS, ATTN_TILE, ATTN_TILE), f32),
                        pltpu.VMEM((N_HEADS, 1, ATTN_TILE), f32),
                        pltpu.VMEM((N_HEADS, LANES, ATTN_TILE), f32)],
        compiler_params=params_2d,
        name="attention",
    )(qt, qaugt, ka, vt)

    half_spec = pl.BlockSpec((TOKEN_TILE, POOL_WIDTH), lambda t: (t, 0))
    out = pl.pallas_call(
        _mix_ffn2_kernel,
        grid=(n_tok,),
        in_specs=[tok_spec, half_spec, half_spec, _resident((1, ATTN_WIDTH)),
                  _resident((D_MODEL, D_MODEL))] + _ffn_weight_specs(),
        out_specs=tok_spec,
        out_shape=jax.ShapeDtypeStruct((T, D_MODEL), f32),
        compiler_params=params_1d,
        name="mix_ffn2",
    )(x1, y_pool.reshape(T, POOL_WIDTH), o_attn.reshape(T, ATTN_WIDTH), row(out_norm_attn),
      w_out.astype(bf16), row(ffn2_norm), ffn2_w_gate.astype(bf16), ffn2_w_up.astype(bf16),
      ffn2_w_down.astype(bf16))
    return out.reshape(B, S, D_MODEL)
```

```python
import math

import numpy as np
import jax
import jax.numpy as jnp
from jax import lax
from jax.experimental import pallas as pl
from jax.experimental.pallas import tpu as pltpu

D_MODEL = 1024
D_FF = 2816
POOL_WIDTH = 512
POOL_GROUPS = 4
POOL_GROUP_DIM = 128
POOL_WINDOWS = (2, 4, 8, 16)
ATTN_WIDTH = 512
HEAD_DIM = 64
N_HEADS = 8
EPS = 1e-6
LOG2E = math.log2(math.e)

LANES = 128
BF16_ROWS = 16
F_ROWS = BF16_ROWS
HEADS_PER_BLOCK = LANES // HEAD_DIM
N_HEAD_BLOCKS = N_HEADS // HEADS_PER_BLOCK
POOL_HALO = 16
N_SPLIT = 3
AUG_STRIDE = 2 * N_SPLIT
QK_DEPTH = 2 * LANES

TOKEN_TILE = 512
FF_CHUNK = 256
ATTN_TILE = 256
VMEM_LIMIT = 56 * 1024 * 1024
MASK_VALUE = -1e30

bf16 = jnp.bfloat16
f32 = jnp.float32


def _rms_scale(x):
    return lax.rsqrt(jnp.mean(x * x, axis=-1, keepdims=True) + EPS)


def _swiglu_residual(x, norm_w, wg_ref, wu_ref, wd_ref):
    h = (x * _rms_scale(x) * norm_w).astype(bf16)
    acc = jnp.zeros(x.shape, f32)
    for c in range(D_FF // FF_CHUNK):
        cols = slice(c * FF_CHUNK, (c + 1) * FF_CHUNK)
        g = jnp.dot(h, wg_ref[:, cols], preferred_element_type=f32)
        u = jnp.dot(h, wu_ref[:, cols], preferred_element_type=f32)
        a = (g * jax.nn.sigmoid(g) * u).astype(bf16)
        acc = acc + jnp.dot(a, wd_ref[cols, :], preferred_element_type=f32)
    return x + 0.5 * acc


def _ffn1_kernel(x_ref, nw_ref, wg_ref, wu_ref, wd_ref, o_ref):
    o_ref[...] = _swiglu_residual(x_ref[...], nw_ref[...], wg_ref, wu_ref, wd_ref)


def _mix_ffn2_kernel(x_ref, yp_ref, oa_ref, nattn_ref, wo_ref, nw_ref, wg_ref, wu_ref, wd_ref, o_ref):
    oa = oa_ref[...]
    ya = (oa * _rms_scale(oa) * nattn_ref[...]).astype(bf16)
    x2 = (x_ref[...]
          + jnp.dot(yp_ref[...], wo_ref[:POOL_WIDTH, :], preferred_element_type=f32)
          + jnp.dot(ya, wo_ref[POOL_WIDTH:, :], preferred_element_type=f32))
    o_ref[...] = _swiglu_residual(x2, nw_ref[...], wg_ref, wu_ref, wd_ref)


def _split3(x):
    hi = x.astype(bf16).astype(f32)
    r = x - hi
    mid = r.astype(bf16).astype(f32)
    lo = (r - mid).astype(bf16).astype(f32)
    return hi, mid, lo


def _in_proj_kernel(x_ref, nw_ref, wpv_ref, wk_ref, wt_ref, bft_ref,
                    poolw_ref, pscale_ref, npool_ref, qn_ref, kn_ref, headsum_ref, triu_ref,
                    selk_ref, ck_ref,
                    yp_ref, qt_ref, qaugt_ref, ka_ref, vt_ref,
                    halo_ref, carry_ref):
    s_idx = pl.program_id(1)
    tm = x_ref.shape[1]

    @pl.when(s_idx == 0)
    def _():
        halo_ref[...] = jnp.zeros(halo_ref.shape, f32)
        carry_ref[...] = jnp.zeros(carry_ref.shape, f32)

    x = x_ref[0]
    h = (x * _rms_scale(x) * nw_ref[...]).astype(bf16)

    t = jnp.dot(h, wk_ref[...], preferred_element_type=f32)
    ssq = jnp.dot((t * t).astype(bf16), headsum_ref[...], preferred_element_type=f32)
    kn = (t * lax.rsqrt(ssq * (1.0 / HEAD_DIM) + EPS) * kn_ref[...]).astype(bf16)

    qvf = lax.dot_general(wt_ref[...], h, (((1,), (1,)), ((), ())), preferred_element_type=f32)
    qt = qvf[:ATTN_WIDTH].reshape(N_HEADS, HEAD_DIM, tm)
    qt = qt * lax.rsqrt(jnp.mean(qt * qt, axis=1, keepdims=True) + EPS)
    qt_ref[0] = (qt.reshape(ATTN_WIDTH, tm) * qn_ref[...]).astype(bf16)
    vt_ref[0] = qvf[ATTN_WIDTH:2 * ATTN_WIDTH].astype(bf16)

    fl = qvf[2 * ATTN_WIDTH:] + bft_ref[...]
    logf = (jnp.minimum(fl, 0.0) - jnp.log1p(jnp.exp(-jnp.abs(fl)))) * LOG2E
    logf = jnp.where(lax.broadcasted_iota(jnp.int32, logf.shape, 0) < N_HEADS, logf, 0.0)
    parts = jnp.concatenate(_split3(logf), axis=0).astype(bf16)
    sums = jnp.dot(parts, triu_ref[...], preferred_element_type=f32)
    fcum_t = (sums[:F_ROWS] + sums[F_ROWS:2 * F_ROWS] + sums[2 * F_ROWS:]) + carry_ref[:, 0:1]
    carry_ref[...] = jnp.broadcast_to(fcum_t[:, tm - 1:tm], carry_ref.shape)
    hi_t, mid_t, lo_t = (part[:N_HEADS] for part in _split3(fcum_t))

    packed_t = jnp.concatenate([hi_t, mid_t, lo_t, jnp.zeros((LANES - N_SPLIT * N_HEADS, tm), f32)], axis=0)
    packed = packed_t.T.astype(bf16)
    kaug = (jnp.dot(packed, selk_ref[...], preferred_element_type=f32) + ck_ref[...]).astype(bf16)
    pieces = []
    for p in range(N_HEAD_BLOCKS):
        cols = slice(p * LANES, (p + 1) * LANES)
        pieces += [kn[:, cols], kaug[:, cols]]
    ka_ref[0] = jnp.concatenate(pieces, axis=1)

    rowid = lax.broadcasted_iota(jnp.int32, (BF16_ROWS, tm), 0)
    for hh in range(N_HEADS):
        base = (hh % HEADS_PER_BLOCK) * AUG_STRIDE
        ones_rows = ((rowid >= base + N_SPLIT) & (rowid < base + 2 * N_SPLIT)).astype(f32)
        aug = jnp.where(rowid == base, hi_t[hh:hh + 1],
                        jnp.where(rowid == base + 1, mid_t[hh:hh + 1],
                                  jnp.where(rowid == base + 2, lo_t[hh:hh + 1], ones_rows)))
        qaugt_ref[0, hh] = aug.astype(bf16)

    pv = jnp.dot(h, wpv_ref[...], preferred_element_type=f32)
    full = jnp.concatenate([halo_ref[...], pv], axis=0)
    halo_ref[...] = pv[tm - POOL_HALO:, :]
    pos = s_idx * tm + lax.broadcasted_iota(jnp.int32, (tm, 1), 0)
    mixed = []
    for g in range(POOL_GROUPS):
        w = POOL_WINDOWS[g]
        cols = slice(g * POOL_GROUP_DIM, (g + 1) * POOL_GROUP_DIM)
        win = full[:, cols]
        shift = 1
        while shift < w:
            win = win + pltpu.roll(win, shift, 0)
            shift *= 2
        count = jnp.minimum(pos + 1, w).astype(f32)
        pooled = win[POOL_HALO:, :] / count - pv[:, cols]
        mixed.append(jnp.dot(pooled.astype(bf16), poolw_ref[g], preferred_element_type=f32))
    mixed = jnp.concatenate(mixed, axis=1) * pscale_ref[...]
    yp_ref[0] = (mixed * _rms_scale(mixed) * npool_ref[...]).astype(bf16)


def _attn_kernel(qt_ref, qaugt_ref, ka_ref, vt_ref, o_ref, vth_ref, qat_ref, s_ref, m_ref, acc_ref):
    i = pl.program_id(1)
    tq = o_ref.shape[1]
    tk = tq

    @pl.when(i == 0)
    def _():
        seq = vt_ref.shape[2]
        ones_then_zeros = (lax.broadcasted_iota(jnp.int32, (HEAD_DIM, seq), 0) == 0).astype(bf16)
        for hh in range(N_HEADS):
            vh = vt_ref[0, hh * HEAD_DIM:(hh + 1) * HEAD_DIM, :]
            odd = hh % HEADS_PER_BLOCK
            vth_ref[hh, 0:HEAD_DIM, :] = ones_then_zeros if odd else vh
            vth_ref[hh, HEAD_DIM:LANES, :] = vh if odd else ones_then_zeros

    zeros_head = jnp.zeros((HEAD_DIM, tq), bf16)
    for hh in range(N_HEADS):
        qh = qt_ref[0, hh * HEAD_DIM:(hh + 1) * HEAD_DIM, :]
        odd = hh % HEADS_PER_BLOCK
        qat_ref[hh, 0:HEAD_DIM, :] = zeros_head if odd else qh
        qat_ref[hh, HEAD_DIM:LANES, :] = qh if odd else zeros_head
        qat_ref[hh, LANES:LANES + BF16_ROWS, :] = qaugt_ref[0, hh]
        qat_ref[hh, LANES + BF16_ROWS:, :] = jnp.zeros((QK_DEPTH - LANES - BF16_ROWS, tq), bf16)

    m_ref[...] = jnp.full(m_ref.shape, MASK_VALUE, f32)
    acc_ref[...] = jnp.zeros(acc_ref.shape, f32)
    key_minus_query = (lax.broadcasted_iota(jnp.int32, (tk, tq), 0)
                       - lax.broadcasted_iota(jnp.int32, (tk, tq), 1))

    def block(j, masked):
        keys = slice(j * tk, (j + 1) * tk)
        slot = j % 2

        def logits(hh):
            p = hh // HEADS_PER_BLOCK
            ka = ka_ref[0, keys, p * QK_DEPTH:(p + 1) * QK_DEPTH]
            st = jnp.dot(ka, qat_ref[hh], preferred_element_type=f32)
            if masked:
                st = jnp.where(key_minus_query > 0, MASK_VALUE, st)
            s_ref[slot, hh] = st

        def softmax(hh):
            m_prev = m_ref[hh]
            m_new = jnp.maximum(m_prev, jnp.max(s_ref[slot, hh], axis=0, keepdims=True))
            m_ref[hh] = m_new
            return jnp.exp2(m_prev - m_new), jnp.exp2(s_ref[slot, hh] - m_new).astype(bf16)

        def values(hh, alpha, pt):
            vt = vth_ref[hh, :, keys]
            acc_ref[hh] = alpha * acc_ref[hh] + jnp.dot(vt, pt, preferred_element_type=f32)

        for hh in range(N_HEADS):
            logits(hh)
        for hh in range(N_HEADS):
            values(hh, *softmax(hh))

    for i_static in range(ka_ref.shape[1] // tk):
        @pl.when(i == i_static)
        def _(n_blocks=i_static + 1):
            for j in range(n_blocks):
                block(j, masked=(j == n_blocks - 1))

    row = lax.broadcasted_iota(jnp.int32, (LANES, tq), 0)
    for p in range(N_HEAD_BLOCKS):
        even, odd = acc_ref[2 * p], acc_ref[2 * p + 1]
        ot = jnp.where(row < HEAD_DIM, even / even[HEAD_DIM:HEAD_DIM + 1, :], odd / odd[0:1, :])
        o_ref[0, :, p * LANES:(p + 1) * LANES] = ot.T


def _resident(shape):
    nd = len(shape)
    return pl.BlockSpec(shape, lambda *_: (0,) * nd, pipeline_mode=pl.Buffered(1))


def _ffn_weight_specs():
    return [_resident((1, D_MODEL)), _resident((D_MODEL, D_FF)), _resident((D_MODEL, D_FF)),
            _resident((D_FF, D_MODEL))]


def _k_aug_constants():
    selk = np.zeros((LANES, N_HEAD_BLOCKS * LANES), np.float32)
    ck = np.zeros((1, N_HEAD_BLOCKS * LANES), np.float32)
    for hh in range(N_HEADS):
        p, odd = divmod(hh, HEADS_PER_BLOCK)
        base = odd * AUG_STRIDE
        for part in range(N_SPLIT):
            ck[0, p * LANES + base + part] = 1.0
            selk[part * N_HEADS + hh, p * LANES + base + N_SPLIT + part] = -1.0
    return jnp.asarray(selk, bf16), jnp.asarray(ck)


def kernel(x, ffn1_norm, ffn1_w_gate, ffn1_w_up, ffn1_w_down, mix_norm, w_in, b_forget, pool_w, pool_scale, q_norm, k_norm, out_norm_pool, out_norm_attn, w_out, ffn2_norm, ffn2_w_gate, ffn2_w_up, ffn2_w_down):
    B, S, D = x.shape
    assert D == D_MODEL and S % TOKEN_TILE == 0 and S % ATTN_TILE == 0
    T = B * S
    n_tok = T // TOKEN_TILE
    row = lambda a: a.reshape(1, -1).astype(f32)

    tok_spec = pl.BlockSpec((TOKEN_TILE, D_MODEL), lambda t: (t, 0))
    params_1d = pltpu.CompilerParams(dimension_semantics=("arbitrary",), vmem_limit_bytes=VMEM_LIMIT)
    params_2d = pltpu.CompilerParams(dimension_semantics=("arbitrary", "arbitrary"), vmem_limit_bytes=VMEM_LIMIT)

    x1 = pl.pallas_call(
        _ffn1_kernel,
        grid=(n_tok,),
        in_specs=[tok_spec] + _ffn_weight_specs(),
        out_specs=tok_spec,
        out_shape=jax.ShapeDtypeStruct((T, D_MODEL), f32),
        compiler_params=params_1d,
        name="ffn1",
    )(x.reshape(T, D_MODEL), row(ffn1_norm), ffn1_w_gate.astype(bf16), ffn1_w_up.astype(bf16),
      ffn1_w_down.astype(bf16))

    c0 = POOL_WIDTH
    w_pv = w_in[:, :c0].astype(bf16)
    w_k = w_in[:, c0 + ATTN_WIDTH:c0 + 2 * ATTN_WIDTH].astype(bf16)
    w_t = jnp.concatenate([w_in[:, c0:c0 + ATTN_WIDTH], w_in[:, c0 + 2 * ATTN_WIDTH:],
                           jnp.zeros((D_MODEL, F_ROWS - N_HEADS), w_in.dtype)], axis=1).T.astype(bf16)
    b_ft = jnp.pad(b_forget.astype(f32), (0, F_ROWS - N_HEADS)).reshape(F_ROWS, 1)
    q_gain = (jnp.tile(q_norm, N_HEADS).astype(f32) * (LOG2E / math.sqrt(HEAD_DIM))).reshape(ATTN_WIDTH, 1)
    k_gain = row(jnp.tile(k_norm, N_HEADS))
    head_id = jnp.arange(ATTN_WIDTH) // HEAD_DIM
    headsum = (head_id[:, None] == head_id[None, :]).astype(bf16)
    triu = (jnp.arange(TOKEN_TILE)[:, None] <= jnp.arange(TOKEN_TILE)[None, :]).astype(bf16)
    selk, ck = _k_aug_constants()

    seq_spec = lambda width: pl.BlockSpec((1, TOKEN_TILE, width), lambda b, s: (b, s, 0))
    y_pool, qt, qaugt, ka, vt = pl.pallas_call(
        _in_proj_kernel,
        grid=(B, S // TOKEN_TILE),
        in_specs=[seq_spec(D_MODEL), _resident((1, D_MODEL)),
                  _resident((D_MODEL, POOL_WIDTH)), _resident((D_MODEL, ATTN_WIDTH)),
                  _resident((2 * ATTN_WIDTH + F_ROWS, D_MODEL)), _resident((F_ROWS, 1)),
                  _resident((POOL_GROUPS, POOL_GROUP_DIM, POOL_GROUP_DIM)), _resident((1, POOL_WIDTH)),
                  _resident((1, POOL_WIDTH)), _resident((ATTN_WIDTH, 1)), _resident((1, ATTN_WIDTH)),
                  _resident((ATTN_WIDTH, ATTN_WIDTH)), _resident((TOKEN_TILE, TOKEN_TILE)),
                  _resident((LANES, N_HEAD_BLOCKS * LANES)), _resident((1, N_HEAD_BLOCKS * LANES))],
        out_specs=[seq_spec(POOL_WIDTH),
                   pl.BlockSpec((1, ATTN_WIDTH, TOKEN_TILE), lambda b, s: (b, 0, s)),
                   pl.BlockSpec((1, N_HEADS, BF16_ROWS, TOKEN_TILE), lambda b, s: (b, 0, 0, s)),
                   seq_spec(N_HEAD_BLOCKS * QK_DEPTH),
                   pl.BlockSpec((1, ATTN_WIDTH, TOKEN_TILE), lambda b, s: (b, 0, s))],
        out_shape=[jax.ShapeDtypeStruct((B, S, POOL_WIDTH), bf16),
                   jax.ShapeDtypeStruct((B, ATTN_WIDTH, S), bf16),
                   jax.ShapeDtypeStruct((B, N_HEADS, BF16_ROWS, S), bf16),
                   jax.ShapeDtypeStruct((B, S, N_HEAD_BLOCKS * QK_DEPTH), bf16),
                   jax.ShapeDtypeStruct((B, ATTN_WIDTH, S), bf16)],
        scratch_shapes=[pltpu.VMEM((POOL_HALO, POOL_WIDTH), f32), pltpu.VMEM((F_ROWS, LANES), f32)],
        compiler_params=params_2d,
        name="in_proj",
    )(x1.reshape(B, S, D_MODEL), row(mix_norm), w_pv, w_k, w_t, b_ft,
      pool_w.astype(bf16), row(pool_scale), row(out_norm_pool), q_gain, k_gain, headsum, triu, selk, ck)

    o_attn = pl.pallas_call(
        _attn_kernel,
        grid=(B, S // ATTN_TILE),
        in_specs=[pl.BlockSpec((1, ATTN_WIDTH, ATTN_TILE), lambda b, i: (b, 0, i)),
                  pl.BlockSpec((1, N_HEADS, BF16_ROWS, ATTN_TILE), lambda b, i: (b, 0, 0, i)),
                  pl.BlockSpec((1, S, N_HEAD_BLOCKS * QK_DEPTH), lambda b, i: (b, 0, 0)),
                  pl.BlockSpec((1, ATTN_WIDTH, S), lambda b, i: (b, 0, 0))],
        out_specs=pl.BlockSpec((1, ATTN_TILE, ATTN_WIDTH), lambda b, i: (b, i, 0)),
        out_shape=jax.ShapeDtypeStruct((B, S, ATTN_WIDTH), f32),
        scratch_shapes=[pltpu.VMEM((N_HEADS, LANES, S), bf16),
                        pltpu.VMEM((N_HEADS, QK_DEPTH, ATTN_TILE), bf16),
                        pltpu.VMEM((2, N_HEADS, ATTN_TILE, ATTN_TILE), f32),
                        pltpu.VMEM((N_HEADS, 1, ATTN_TILE), f32),
                        pltpu.VMEM((N_HEADS, LANES, ATTN_TILE), f32)],
        compiler_params=params_2d,
        name="attention",
    )(qt, qaugt, ka, vt)

    half_spec = pl.BlockSpec((TOKEN_TILE, POOL_WIDTH), lambda t: (t, 0))
    out = pl.pallas_call(
        _mix_ffn2_kernel,
        grid=(n_tok,),
        in_specs=[tok_spec, half_spec, half_spec, _resident((1, ATTN_WIDTH)),
                  _resident((D_MODEL, D_MODEL))] + _ffn_weight_specs(),
        out_specs=tok_spec,
        out_shape=jax.ShapeDtypeStruct((T, D_MODEL), f32),
        compiler_params=params_1d,
        name="mix_ffn2",
    )(x1, y_pool.reshape(T, POOL_WIDTH), o_attn.reshape(T, ATTN_WIDTH), row(out_norm_attn),
      w_out.astype(bf16), row(ffn2_norm), ffn2_w_gate.astype(bf16), ffn2_w_up.astype(bf16),
      ffn2_w_down.astype(bf16))
    return out.reshape(B, S, D_MODEL)
```

```python
import math

import numpy as np
import jax
import jax.numpy as jnp
from jax import lax
from jax.experimental import pallas as pl
from jax.experimental.pallas import tpu as pltpu

D_MODEL = 1024
D_FF = 2816
POOL_WIDTH = 512
POOL_GROUPS = 4
POOL_GROUP_DIM = 128
POOL_WINDOWS = (2, 4, 8, 16)
ATTN_WIDTH = 512
HEAD_DIM = 64
N_HEADS = 8
EPS = 1e-6
LOG2E = math.log2(math.e)

LANES = 128
BF16_ROWS = 16
F_ROWS = BF16_ROWS
HEADS_PER_BLOCK = LANES // HEAD_DIM
N_HEAD_BLOCKS = N_HEADS // HEADS_PER_BLOCK
POOL_HALO = 16
N_SPLIT = 3
AUG_STRIDE = 2 * N_SPLIT
QK_DEPTH = 2 * LANES

TOKEN_TILE = 512
FF_CHUNK = 256
ATTN_TILE = 256
VMEM_LIMIT = 56 * 1024 * 1024
MASK_VALUE = -1e30

bf16 = jnp.bfloat16
f32 = jnp.float32


def _rms_scale(x):
    return lax.rsqrt(jnp.mean(x * x, axis=-1, keepdims=True) + EPS)


def _swiglu_residual(x, norm_w, wg_ref, wu_ref, wd_ref):
    h = (x * norm_w).astype(bf16)
    r = _rms_scale(x)
    acc = jnp.zeros(x.shape, f32)
    for c in range(D_FF // FF_CHUNK):
        cols = slice(c * FF_CHUNK, (c + 1) * FF_CHUNK)
        g = r * jnp.dot(h, wg_ref[:, cols], preferred_element_type=f32)
        u = jnp.dot(h, wu_ref[:, cols], preferred_element_type=f32)
        a = (g * jax.nn.sigmoid(g) * u).astype(bf16)
        acc = acc + jnp.dot(a, wd_ref[cols, :], preferred_element_type=f32)
    return x + (0.5 * r) * acc


def _ffn1_kernel(x_ref, nw_ref, wg_ref, wu_ref, wd_ref, o_ref):
    o_ref[...] = _swiglu_residual(x_ref[...], nw_ref[...], wg_ref, wu_ref, wd_ref)


def _mix_ffn2_kernel(x_ref, yp_ref, oa_ref, nattn_ref, wo_ref, nw_ref, wg_ref, wu_ref, wd_ref, o_ref):
    oa = oa_ref[...]
    ya = (oa * _rms_scale(oa) * nattn_ref[...]).astype(bf16)
    x2 = (x_ref[...]
          + jnp.dot(yp_ref[...], wo_ref[:POOL_WIDTH, :], preferred_element_type=f32)
          + jnp.dot(ya, wo_ref[POOL_WIDTH:, :], preferred_element_type=f32))
    o_ref[...] = _swiglu_residual(x2, nw_ref[...], wg_ref, wu_ref, wd_ref)


def _split3(x):
    hi = x.astype(bf16).astype(f32)
    r = x - hi
    mid = r.astype(bf16).astype(f32)
    lo = (r - mid).astype(bf16).astype(f32)
    return hi, mid, lo


def _in_proj_kernel(x_ref, nw_ref, wpv_ref, wk_ref, wt_ref, bft_ref,
                    poolw_ref, pscale_ref, npool_ref, qn_ref, kn_ref, headsum_ref, triu_ref,
                    selk_ref, ck_ref,
                    yp_ref, qt_ref, qaugt_ref, ka_ref, vt_ref,
                    halo_ref, carry_ref):
    s_idx = pl.program_id(1)
    tm = x_ref.shape[1]

    @pl.when(s_idx == 0)
    def _():
        halo_ref[...] = jnp.zeros(halo_ref.shape, f32)
        carry_ref[...] = jnp.zeros(carry_ref.shape, f32)

    x = x_ref[0]
    h = (x * nw_ref[...]).astype(bf16)
    r = _rms_scale(x)
    r_row = jnp.broadcast_to(r, (tm, LANES)).T[0:1]

    qvf = r_row * lax.dot_general(wt_ref[...], h, (((1,), (1,)), ((), ())), preferred_element_type=f32)
    pv = r * jnp.dot(h, wpv_ref[...], preferred_element_type=f32)
    t = r * jnp.dot(h, wk_ref[...], preferred_element_type=f32)

    fl = qvf[2 * ATTN_WIDTH:] + bft_ref[...]
    logf = (jnp.minimum(fl, 0.0) - jnp.log1p(jnp.exp(-jnp.abs(fl)))) * LOG2E
    logf = jnp.where(lax.broadcasted_iota(jnp.int32, logf.shape, 0) < N_HEADS, logf, 0.0)
    parts = jnp.concatenate(_split3(logf), axis=0).astype(bf16)
    sums = jnp.dot(parts, triu_ref[...], preferred_element_type=f32)
    fcum_t = (sums[:F_ROWS] + sums[F_ROWS:2 * F_ROWS] + sums[2 * F_ROWS:]) + carry_ref[:, 0:1]
    carry_ref[...] = jnp.broadcast_to(fcum_t[:, tm - 1:tm], carry_ref.shape)
    hi_t, mid_t, lo_t = (part[:N_HEADS] for part in _split3(fcum_t))

    full = jnp.concatenate([halo_ref[...], pv], axis=0)
    halo_ref[...] = pv[tm - POOL_HALO:, :]
    pos = s_idx * tm + lax.broadcasted_iota(jnp.int32, (tm, 1), 0)
    mixed = []
    for g in range(POOL_GROUPS):
        w = POOL_WINDOWS[g]
        cols = slice(g * POOL_GROUP_DIM, (g + 1) * POOL_GROUP_DIM)
        win = full[:, cols]
        shift = 1
        while shift < w:
            win = win + pltpu.roll(win, shift, 0)
            shift *= 2
        count = jnp.minimum(pos + 1, w).astype(f32)
        pooled = win[POOL_HALO:, :] / count - pv[:, cols]
        mixed.append(jnp.dot(pooled.astype(bf16), poolw_ref[g], preferred_element_type=f32))
    mixed = jnp.concatenate(mixed, axis=1) * pscale_ref[...]
    yp_ref[0] = (mixed * _rms_scale(mixed) * npool_ref[...]).astype(bf16)

    ssq = jnp.dot((t * t).astype(bf16), headsum_ref[...], preferred_element_type=f32)
    kn = (t * lax.rsqrt(ssq * (1.0 / HEAD_DIM) + EPS) * kn_ref[...]).astype(bf16)

    packed_t = jnp.concatenate([hi_t, mid_t, lo_t, jnp.zeros((LANES - N_SPLIT * N_HEADS, tm), f32)], axis=0)
    packed = packed_t.T.astype(bf16)
    kaug = (jnp.dot(packed, selk_ref[...], preferred_element_type=f32) + ck_ref[...]).astype(bf16)
    pieces = []
    for p in range(N_HEAD_BLOCKS):
        cols = slice(p * LANES, (p + 1) * LANES)
        pieces += [kn[:, cols], kaug[:, cols]]
    ka_ref[0] = jnp.concatenate(pieces, axis=1)

    qt = qvf[:ATTN_WIDTH].reshape(N_HEADS, HEAD_DIM, tm)
    qt = qt * lax.rsqrt(jnp.mean(qt * qt, axis=1, keepdims=True) + EPS)
    qt_ref[0] = (qt.reshape(ATTN_WIDTH, tm) * qn_ref[...]).astype(bf16)
    vt_ref[0] = qvf[ATTN_WIDTH:2 * ATTN_WIDTH].astype(bf16)

    rowid = lax.broadcasted_iota(jnp.int32, (BF16_ROWS, tm), 0)
    for hh in range(N_HEADS):
        base = (hh % HEADS_PER_BLOCK) * AUG_STRIDE
        ones_rows = ((rowid >= base + N_SPLIT) & (rowid < base + 2 * N_SPLIT)).astype(f32)
        aug = jnp.where(rowid == base, hi_t[hh:hh + 1],
                        jnp.where(rowid == base + 1, mid_t[hh:hh + 1],
                                  jnp.where(rowid == base + 2, lo_t[hh:hh + 1], ones_rows)))
        qaugt_ref[0, hh] = aug.astype(bf16)


def _attn_kernel(qt_ref, qaugt_ref, ka_ref, vt_ref, o_ref, vth_ref, qat_ref, s_ref, m_ref, acc_ref):
    i = pl.program_id(1)
    tq = o_ref.shape[1]
    tk = tq

    @pl.when(i == 0)
    def _():
        seq = vt_ref.shape[2]
        ones_then_zeros = (lax.broadcasted_iota(jnp.int32, (HEAD_DIM, seq), 0) == 0).astype(bf16)
        for hh in range(N_HEADS):
            vh = vt_ref[0, hh * HEAD_DIM:(hh + 1) * HEAD_DIM, :]
            odd = hh % HEADS_PER_BLOCK
            vth_ref[hh, 0:HEAD_DIM, :] = ones_then_zeros if odd else vh
            vth_ref[hh, HEAD_DIM:LANES, :] = vh if odd else ones_then_zeros

    zeros_head = jnp.zeros((HEAD_DIM, tq), bf16)
    for hh in range(N_HEADS):
        qh = qt_ref[0, hh * HEAD_DIM:(hh + 1) * HEAD_DIM, :]
        odd = hh % HEADS_PER_BLOCK
        qat_ref[hh, 0:HEAD_DIM, :] = zeros_head if odd else qh
        qat_ref[hh, HEAD_DIM:LANES, :] = qh if odd else zeros_head
        qat_ref[hh, LANES:LANES + BF16_ROWS, :] = qaugt_ref[0, hh]
        qat_ref[hh, LANES + BF16_ROWS:, :] = jnp.zeros((QK_DEPTH - LANES - BF16_ROWS, tq), bf16)

    m_ref[...] = jnp.full(m_ref.shape, MASK_VALUE, f32)
    acc_ref[...] = jnp.zeros(acc_ref.shape, f32)
    key_minus_query = (lax.broadcasted_iota(jnp.int32, (tk, tq), 0)
                       - lax.broadcasted_iota(jnp.int32, (tk, tq), 1))

    def keys(j):
        return slice(j * tk, (j + 1) * tk)

    def logits(j, hh, masked):
        p = hh // HEADS_PER_BLOCK
        ka = ka_ref[0, keys(j), p * QK_DEPTH:(p + 1) * QK_DEPTH]
        st = jnp.dot(ka, qat_ref[hh], preferred_element_type=f32)
        if masked:
            st = jnp.where(key_minus_query > 0, MASK_VALUE, st)
        s_ref[j % 2, hh] = st

    def softmax_values(j, hh):
        st_ref = s_ref.at[j % 2, hh]
        m_prev = m_ref[hh]
        m_new = jnp.maximum(m_prev, jnp.max(st_ref[...], axis=0, keepdims=True))
        m_ref[hh] = m_new
        alpha = jnp.exp2(m_prev - m_new)
        pt = jnp.exp2(st_ref[...] - m_new).astype(bf16)
        vt = vth_ref[hh, :, keys(j)]
        acc_ref[hh] = alpha * acc_ref[hh] + jnp.dot(vt, pt, preferred_element_type=f32)

    for i_static in range(ka_ref.shape[1] // tk):
        @pl.when(i == i_static)
        def _(n_blocks=i_static + 1):
            for hh in range(N_HEADS):
                logits(0, hh, masked=(n_blocks == 1))
            for j in range(n_blocks):
                for hh in range(N_HEADS):
                    softmax_values(j, hh)
                    if j + 1 < n_blocks:
                        logits(j + 1, hh, masked=(j + 2 == n_blocks))

    row = lax.broadcasted_iota(jnp.int32, (LANES, tq), 0)
    for p in range(N_HEAD_BLOCKS):
        even, odd = acc_ref[2 * p], acc_ref[2 * p + 1]
        ot = jnp.where(row < HEAD_DIM, even / even[HEAD_DIM:HEAD_DIM + 1, :], odd / odd[0:1, :])
        o_ref[0, :, p * LANES:(p + 1) * LANES] = ot.T


def _resident(shape):
    nd = len(shape)
    return pl.BlockSpec(shape, lambda *_: (0,) * nd, pipeline_mode=pl.Buffered(1))


def _ffn_weight_specs():
    return [_resident((1, D_MODEL)), _resident((D_MODEL, D_FF)), _resident((D_MODEL, D_FF)),
            _resident((D_FF, D_MODEL))]


def _k_aug_constants():
    selk = np.zeros((LANES, N_HEAD_BLOCKS * LANES), np.float32)
    ck = np.zeros((1, N_HEAD_BLOCKS * LANES), np.float32)
    for hh in range(N_HEADS):
        p, odd = divmod(hh, HEADS_PER_BLOCK)
        base = odd * AUG_STRIDE
        for part in range(N_SPLIT):
            ck[0, p * LANES + base + part] = 1.0
            selk[part * N_HEADS + hh, p * LANES + base + N_SPLIT + part] = -1.0
    return jnp.asarray(selk, bf16), jnp.asarray(ck)


def kernel(x, ffn1_norm, ffn1_w_gate, ffn1_w_up, ffn1_w_down, mix_norm, w_in, b_forget, pool_w, pool_scale, q_norm, k_norm, out_norm_pool, out_norm_attn, w_out, ffn2_norm, ffn2_w_gate, ffn2_w_up, ffn2_w_down):
    B, S, D = x.shape
    assert D == D_MODEL and S % TOKEN_TILE == 0 and S % ATTN_TILE == 0
    T = B * S
    n_tok = T // TOKEN_TILE
    row = lambda a: a.reshape(1, -1).astype(f32)

    tok_spec = pl.BlockSpec((TOKEN_TILE, D_MODEL), lambda t: (t, 0))
    params_1d = pltpu.CompilerParams(dimension_semantics=("arbitrary",), vmem_limit_bytes=VMEM_LIMIT)
    params_2d = pltpu.CompilerParams(dimension_semantics=("arbitrary", "arbitrary"), vmem_limit_bytes=VMEM_LIMIT)

    x1 = pl.pallas_call(
        _ffn1_kernel,
        grid=(n_tok,),
        in_specs=[tok_spec] + _ffn_weight_specs(),
        out_specs=tok_spec,
        out_shape=jax.ShapeDtypeStruct((T, D_MODEL), f32),
        compiler_params=params_1d,
        name="ffn1",
    )(x.reshape(T, D_MODEL), row(ffn1_norm), ffn1_w_gate.astype(bf16), ffn1_w_up.astype(bf16),
      ffn1_w_down.astype(bf16))

    c0 = POOL_WIDTH
    w_pv = w_in[:, :c0].astype(bf16)
    w_k = w_in[:, c0 + ATTN_WIDTH:c0 + 2 * ATTN_WIDTH].astype(bf16)
    w_t = jnp.concatenate([w_in[:, c0:c0 + ATTN_WIDTH], w_in[:, c0 + 2 * ATTN_WIDTH:],
                           jnp.zeros((D_MODEL, F_ROWS - N_HEADS), w_in.dtype)], axis=1).T.astype(bf16)
    b_ft = jnp.pad(b_forget.astype(f32), (0, F_ROWS - N_HEADS)).reshape(F_ROWS, 1)
    q_gain = (jnp.tile(q_norm, N_HEADS).astype(f32) * (LOG2E / math.sqrt(HEAD_DIM))).reshape(ATTN_WIDTH, 1)
    k_gain = row(jnp.tile(k_norm, N_HEADS))
    head_id = jnp.arange(ATTN_WIDTH) // HEAD_DIM
    headsum = (head_id[:, None] == head_id[None, :]).astype(bf16)
    triu = (jnp.arange(TOKEN_TILE)[:, None] <= jnp.arange(TOKEN_TILE)[None, :]).astype(bf16)
    selk, ck = _k_aug_constants()

    seq_spec = lambda width: pl.BlockSpec((1, TOKEN_TILE, width), lambda b, s: (b, s, 0))
    y_pool, qt, qaugt, ka, vt = pl.pallas_call(
        _in_proj_kernel,
        grid=(B, S // TOKEN_TILE),
        in_specs=[seq_spec(D_MODEL), _resident((1, D_MODEL)),
                  _resident((D_MODEL, POOL_WIDTH)), _resident((D_MODEL, ATTN_WIDTH)),
                  _resident((2 * ATTN_WIDTH + F_ROWS, D_MODEL)), _resident((F_ROWS, 1)),
                  _resident((POOL_GROUPS, POOL_GROUP_DIM, POOL_GROUP_DIM)), _resident((1, POOL_WIDTH)),
                  _resident((1, POOL_WIDTH)), _resident((ATTN_WIDTH, 1)), _resident((1, ATTN_WIDTH)),
                  _resident((ATTN_WIDTH, ATTN_WIDTH)), _resident((TOKEN_TILE, TOKEN_TILE)),
                  _resident((LANES, N_HEAD_BLOCKS * LANES)), _resident((1, N_HEAD_BLOCKS * LANES))],
        out_specs=[seq_spec(POOL_WIDTH),
                   pl.BlockSpec((1, ATTN_WIDTH, TOKEN_TILE), lambda b, s: (b, 0, s)),
                   pl.BlockSpec((1, N_HEADS, BF16_ROWS, TOKEN_TILE), lambda b, s: (b, 0, 0, s)),
                   seq_spec(N_HEAD_BLOCKS * QK_DEPTH),
                   pl.BlockSpec((1, ATTN_WIDTH, TOKEN_TILE), lambda b, s: (b, 0, s))],
        out_shape=[jax.ShapeDtypeStruct((B, S, POOL_WIDTH), bf16),
                   jax.ShapeDtypeStruct((B, ATTN_WIDTH, S), bf16),
                   jax.ShapeDtypeStruct((B, N_HEADS, BF16_ROWS, S), bf16),
                   jax.ShapeDtypeStruct((B, S, N_HEAD_BLOCKS * QK_DEPTH), bf16),
                   jax.ShapeDtypeStruct((B, ATTN_WIDTH, S), bf16)],
        scratch_shapes=[pltpu.VMEM((POOL_HALO, POOL_WIDTH), f32), pltpu.VMEM((F_ROWS, LANES), f32)],
        compiler_params=params_2d,
        name="in_proj",
    )(x1.reshape(B, S, D_MODEL), row(mix_norm), w_pv, w_k, w_t, b_ft,
      pool_w.astype(bf16), row(pool_scale), row(out_norm_pool), q_gain, k_gain, headsum, triu, selk, ck)

    o_attn = pl.pallas_call(
        _attn_kernel,
        grid=(B, S // ATTN_TILE),
        in_specs=[pl.BlockSpec((1, ATTN_WIDTH, ATTN_TILE), lambda b, i: (b, 0, i)),
                  pl.BlockSpec((1, N_HEADS, BF16_ROWS, ATTN_TILE), lambda b, i: (b, 0, 0, i)),
                  pl.BlockSpec((1, S, N_HEAD_BLOCKS * QK_DEPTH), lambda b, i: (b, 0, 0)),
                  pl.BlockSpec((1, ATTN_WIDTH, S), lambda b, i: (b, 0, 0))],
        out_specs=pl.BlockSpec((1, ATTN_TILE, ATTN_WIDTH), lambda b, i: (b, i, 0)),
        out_shape=jax.ShapeDtypeStruct((B, S, ATTN_WIDTH), f32),
        scratch_shapes=[pltpu.VMEM((N_HEADS, LANES, S), bf16),
                        pltpu.VMEM((N_HEADS, QK_DEPTH, ATTN_TILE), bf16),
                        pltpu.VMEM((2, N_HEADS, ATTN_TILE, ATTN_TILE), f32),
                        pltpu.VMEM((N_HEADS, 1, ATTN_TILE), f32),
                        pltpu.VMEM((N_HEADS, LANES, ATTN_TILE), f32)],
        compiler_params=params_2d,
        name="attention",
    )(qt, qaugt, ka, vt)

    half_spec = pl.BlockSpec((TOKEN_TILE, POOL_WIDTH), lambda t: (t, 0))
    out = pl.pallas_call(
        _mix_ffn2_kernel,
        grid=(n_tok,),
        in_specs=[tok_spec, half_spec, half_spec, _resident((1, ATTN_WIDTH)),
                  _resident((D_MODEL, D_MODEL))] + _ffn_weight_specs(),
        out_specs=tok_spec,
        out_shape=jax.ShapeDtypeStruct((T, D_MODEL), f32),
        compiler_params=params_1d,
        name="mix_ffn2",
    )(x1, y_pool.reshape(T, POOL_WIDTH), o_attn.reshape(T, ATTN_WIDTH), row(out_norm_attn),
      w_out.astype(bf16), row(ffn2_norm), ffn2_w_gate.astype(bf16), ffn2_w_up.astype(bf16),
      ffn2_w_down.astype(bf16))
    return out.reshape(B, S, D_MODEL)
```

```python
import math

import numpy as np
import jax
import jax.numpy as jnp
from jax import lax
from jax.experimental import pallas as pl
from jax.experimental.pallas import tpu as pltpu

D_MODEL = 1024
D_FF = 2816
POOL_WIDTH = 512
POOL_GROUPS = 4
POOL_GROUP_DIM = 128
POOL_WINDOWS = (2, 4, 8, 16)
ATTN_WIDTH = 512
HEAD_DIM = 64
N_HEADS = 8
EPS = 1e-6
LOG2E = math.log2(math.e)

LANES = 128
BF16_ROWS = 16
F_ROWS = BF16_ROWS
HEADS_PER_BLOCK = LANES // HEAD_DIM
N_HEAD_BLOCKS = N_HEADS // HEADS_PER_BLOCK
POOL_HALO = 16
N_SPLIT = 3
AUG_STRIDE = 2 * N_SPLIT
QK_DEPTH = 2 * LANES

TOKEN_TILE = 512
FF_CHUNK = 256
ATTN_TILE = 256
VMEM_LIMIT = 56 * 1024 * 1024
MASK_VALUE = -1e30

bf16 = jnp.bfloat16
f32 = jnp.float32


def _rms_scale(x):
    return lax.rsqrt(jnp.mean(x * x, axis=-1, keepdims=True) + EPS)


def _swiglu_residual(x, norm_w, wg_ref, wu_ref, wd_ref):
    h = (x * norm_w).astype(bf16)
    r = _rms_scale(x)
    acc = jnp.zeros(x.shape, f32)
    for c in range(D_FF // FF_CHUNK):
        cols = slice(c * FF_CHUNK, (c + 1) * FF_CHUNK)
        g = r * jnp.dot(h, wg_ref[:, cols], preferred_element_type=f32)
        u = jnp.dot(h, wu_ref[:, cols], preferred_element_type=f32)
        a = (g * jax.nn.sigmoid(g) * u).astype(bf16)
        acc = acc + jnp.dot(a, wd_ref[cols, :], preferred_element_type=f32)
    return x + (0.5 * r) * acc


def _ffn1_kernel(x_ref, nw_ref, wg_ref, wu_ref, wd_ref, o_ref):
    o_ref[...] = _swiglu_residual(x_ref[...], nw_ref[...], wg_ref, wu_ref, wd_ref)


def _mix_ffn2_kernel(x_ref, yp_ref, oa_ref, nattn_ref, wo_ref, nw_ref, wg_ref, wu_ref, wd_ref, o_ref):
    oa = oa_ref[...]
    ya = (oa * _rms_scale(oa) * nattn_ref[...]).astype(bf16)
    x2 = (x_ref[...]
          + jnp.dot(yp_ref[...], wo_ref[:POOL_WIDTH, :], preferred_element_type=f32)
          + jnp.dot(ya, wo_ref[POOL_WIDTH:, :], preferred_element_type=f32))
    o_ref[...] = _swiglu_residual(x2, nw_ref[...], wg_ref, wu_ref, wd_ref)


def _split3(x):
    hi = x.astype(bf16).astype(f32)
    r = x - hi
    mid = r.astype(bf16).astype(f32)
    lo = (r - mid).astype(bf16).astype(f32)
    return hi, mid, lo


def _in_proj_kernel(x_ref, nw_ref, wpv_ref, wk_ref, wt_ref, bft_ref,
                    poolw_ref, pscale_ref, npool_ref, qn_ref, kn_ref, headsum_ref, triu_ref,
                    selk_ref, ck_ref,
                    yp_ref, qt_ref, qaugt_ref, ka_ref, vt_ref,
                    halo_ref, carry_ref):
    s_idx = pl.program_id(1)
    tm = x_ref.shape[1]

    @pl.when(s_idx == 0)
    def _():
        halo_ref[...] = jnp.zeros(halo_ref.shape, f32)
        carry_ref[...] = jnp.zeros(carry_ref.shape, f32)

    x = x_ref[0]
    h = (x * nw_ref[...]).astype(bf16)
    r = _rms_scale(x)
    r_row = jnp.broadcast_to(r, (tm, LANES)).T[0:1]

    qvf = r_row * lax.dot_general(wt_ref[...], h, (((1,), (1,)), ((), ())), preferred_element_type=f32)
    pv = r * jnp.dot(h, wpv_ref[...], preferred_element_type=f32)
    t = r * jnp.dot(h, wk_ref[...], preferred_element_type=f32)

    fl = qvf[2 * ATTN_WIDTH:] + bft_ref[...]
    logf = (jnp.minimum(fl, 0.0) - jnp.log1p(jnp.exp(-jnp.abs(fl)))) * LOG2E
    logf = jnp.where(lax.broadcasted_iota(jnp.int32, logf.shape, 0) < N_HEADS, logf, 0.0)
    parts = jnp.concatenate(_split3(logf), axis=0).astype(bf16)
    sums = jnp.dot(parts, triu_ref[...], preferred_element_type=f32)
    fcum_t = (sums[:F_ROWS] + sums[F_ROWS:2 * F_ROWS] + sums[2 * F_ROWS:]) + carry_ref[:, 0:1]
    carry_ref[...] = jnp.broadcast_to(fcum_t[:, tm - 1:tm], carry_ref.shape)
    hi_t, mid_t, lo_t = (part[:N_HEADS] for part in _split3(fcum_t))

    full = jnp.concatenate([halo_ref[...], pv], axis=0)
    halo_ref[...] = pv[tm - POOL_HALO:, :]
    pos = s_idx * tm + lax.broadcasted_iota(jnp.int32, (tm, 1), 0)
    mixed = []
    for g in range(POOL_GROUPS):
        w = POOL_WINDOWS[g]
        cols = slice(g * POOL_GROUP_DIM, (g + 1) * POOL_GROUP_DIM)
        win = full[:, cols]
        shift = 1
        while shift < w:
            win = win + pltpu.roll(win, shift, 0)
            shift *= 2
        count = jnp.minimum(pos + 1, w).astype(f32)
        pooled = win[POOL_HALO:, :] / count - pv[:, cols]
        mixed.append(jnp.dot(pooled.astype(bf16), poolw_ref[g], preferred_element_type=f32))
    mixed = jnp.concatenate(mixed, axis=1) * pscale_ref[...]
    yp_ref[0] = (mixed * _rms_scale(mixed) * npool_ref[...]).astype(bf16)

    ssq = jnp.dot((t * t).astype(bf16), headsum_ref[...], preferred_element_type=f32)
    kn = (t * lax.rsqrt(ssq * (1.0 / HEAD_DIM) + EPS) * kn_ref[...]).astype(bf16)

    packed_t = jnp.concatenate([hi_t, mid_t, lo_t, jnp.zeros((LANES - N_SPLIT * N_HEADS, tm), f32)], axis=0)
    packed = packed_t.T.astype(bf16)
    kaug = (jnp.dot(packed, selk_ref[...], preferred_element_type=f32) + ck_ref[...]).astype(bf16)
    pieces = []
    for p in range(N_HEAD_BLOCKS):
        cols = slice(p * LANES, (p + 1) * LANES)
        pieces += [kn[:, cols], kaug[:, cols]]
    ka_ref[0] = jnp.concatenate(pieces, axis=1)

    qt = qvf[:ATTN_WIDTH].reshape(N_HEADS, HEAD_DIM, tm)
    qt = qt * lax.rsqrt(jnp.mean(qt * qt, axis=1, keepdims=True) + EPS)
    qt_ref[0] = (qt.reshape(ATTN_WIDTH, tm) * qn_ref[...]).astype(bf16)
    vt_ref[0] = qvf[ATTN_WIDTH:2 * ATTN_WIDTH].astype(bf16)

    rowid = lax.broadcasted_iota(jnp.int32, (BF16_ROWS, tm), 0)
    for hh in range(N_HEADS):
        base = (hh % HEADS_PER_BLOCK) * AUG_STRIDE
        ones_rows = ((rowid >= base + N_SPLIT) & (rowid < base + 2 * N_SPLIT)).astype(f32)
        aug = jnp.where(rowid == base, hi_t[hh:hh + 1],
                        jnp.where(rowid == base + 1, mid_t[hh:hh + 1],
                                  jnp.where(rowid == base + 2, lo_t[hh:hh + 1], ones_rows)))
        qaugt_ref[0, hh] = aug.astype(bf16)


def _attn_kernel(qt_ref, qaugt_ref, ka_ref, vt_ref, o_ref, vth_ref, qat_ref, s_ref, m_ref, acc_ref):
    i = pl.program_id(1)
    tq = o_ref.shape[1]
    tk = tq

    def build_values_operand(hh):
        seq = vt_ref.shape[2]
        ones_then_zeros = (lax.broadcasted_iota(jnp.int32, (HEAD_DIM, seq), 0) == 0).astype(bf16)
        vh = vt_ref[0, hh * HEAD_DIM:(hh + 1) * HEAD_DIM, :]
        odd = hh % HEADS_PER_BLOCK
        vth_ref[hh, 0:HEAD_DIM, :] = ones_then_zeros if odd else vh
        vth_ref[hh, HEAD_DIM:LANES, :] = vh if odd else ones_then_zeros

    def build_query_operand(hh):
        zeros_head = jnp.zeros((HEAD_DIM, tq), bf16)
        qh = qt_ref[0, hh * HEAD_DIM:(hh + 1) * HEAD_DIM, :]
        odd = hh % HEADS_PER_BLOCK
        qat_ref[hh, 0:HEAD_DIM, :] = zeros_head if odd else qh
        qat_ref[hh, HEAD_DIM:LANES, :] = qh if odd else zeros_head
        qat_ref[hh, LANES:LANES + BF16_ROWS, :] = qaugt_ref[0, hh]
        qat_ref[hh, LANES + BF16_ROWS:, :] = jnp.zeros((QK_DEPTH - LANES - BF16_ROWS, tq), bf16)

    def keys(j):
        return slice(j * tk, (j + 1) * tk)

    def logits(j, hh, masked):
        p = hh // HEADS_PER_BLOCK
        ka = ka_ref[0, keys(j), p * QK_DEPTH:(p + 1) * QK_DEPTH]
        st = jnp.dot(ka, qat_ref[hh], preferred_element_type=f32)
        if masked:
            key_minus_query = (lax.broadcasted_iota(jnp.int32, (tk, tq), 0)
                               - lax.broadcasted_iota(jnp.int32, (tk, tq), 1))
            st = jnp.where(key_minus_query > 0, MASK_VALUE, st)
        s_ref[j % 2, hh] = st

    def softmax_values(j, hh):
        st_ref = s_ref.at[j % 2, hh]
        vt = vth_ref[hh, :, keys(j)]
        m_blk = jnp.max(st_ref[...], axis=0, keepdims=True)
        if j == 0:
            m_ref[hh] = m_blk
            pt = jnp.exp2(st_ref[...] - m_blk).astype(bf16)
            acc_ref[hh] = jnp.dot(vt, pt, preferred_element_type=f32)
        else:
            m_prev = m_ref[hh]
            m_new = jnp.maximum(m_prev, m_blk)
            m_ref[hh] = m_new
            pt = jnp.exp2(st_ref[...] - m_new).astype(bf16)
            acc_ref[hh] = jnp.exp2(m_prev - m_new) * acc_ref[hh] + jnp.dot(vt, pt, preferred_element_type=f32)

    def write_output(p):
        row = lax.broadcasted_iota(jnp.int32, (LANES, tq), 0)
        even, odd = acc_ref[2 * p], acc_ref[2 * p + 1]
        ot = jnp.where(row < HEAD_DIM, even / even[HEAD_DIM:HEAD_DIM + 1, :], odd / odd[0:1, :])
        o_ref[0, :, p * LANES:(p + 1) * LANES] = ot.T

    for i_static in range(ka_ref.shape[1] // tk):
        @pl.when(i == i_static)
        def _(n_blocks=i_static + 1):
            for hh in range(N_HEADS):
                build_query_operand(hh)
                logits(0, hh, masked=(n_blocks == 1))
            for j in range(n_blocks):
                for hh in range(N_HEADS):
                    if n_blocks == 1:
                        build_values_operand(hh)
                    softmax_values(j, hh)
                    if j + 1 < n_blocks:
                        logits(j + 1, hh, masked=(j + 2 == n_blocks))
                    elif hh % HEADS_PER_BLOCK == HEADS_PER_BLOCK - 1:
                        write_output(hh // HEADS_PER_BLOCK)


def _resident(shape):
    nd = len(shape)
    return pl.BlockSpec(shape, lambda *_: (0,) * nd, pipeline_mode=pl.Buffered(1))


def _ffn_weight_specs():
    return [_resident((1, D_MODEL)), _resident((D_MODEL, D_FF)), _resident((D_MODEL, D_FF)),
            _resident((D_FF, D_MODEL))]


def _k_aug_constants():
    selk = np.zeros((LANES, N_HEAD_BLOCKS * LANES), np.float32)
    ck = np.zeros((1, N_HEAD_BLOCKS * LANES), np.float32)
    for hh in range(N_HEADS):
        p, odd = divmod(hh, HEADS_PER_BLOCK)
        base = odd * AUG_STRIDE
        for part in range(N_SPLIT):
            ck[0, p * LANES + base + part] = 1.0
            selk[part * N_HEADS + hh, p * LANES + base + N_SPLIT + part] = -1.0
    return jnp.asarray(selk, bf16), jnp.asarray(ck)


def kernel(x, ffn1_norm, ffn1_w_gate, ffn1_w_up, ffn1_w_down, mix_norm, w_in, b_forget, pool_w, pool_scale, q_norm, k_norm, out_norm_pool, out_norm_attn, w_out, ffn2_norm, ffn2_w_gate, ffn2_w_up, ffn2_w_down):
    B, S, D = x.shape
    assert D == D_MODEL and S % TOKEN_TILE == 0 and S % ATTN_TILE == 0
    T = B * S
    n_tok = T // TOKEN_TILE
    row = lambda a: a.reshape(1, -1).astype(f32)

    tok_spec = pl.BlockSpec((TOKEN_TILE, D_MODEL), lambda t: (t, 0))
    params_1d = pltpu.CompilerParams(dimension_semantics=("arbitrary",), vmem_limit_bytes=VMEM_LIMIT)
    params_2d = pltpu.CompilerParams(dimension_semantics=("arbitrary", "arbitrary"), vmem_limit_bytes=VMEM_LIMIT)

    x1 = pl.pallas_call(
        _ffn1_kernel,
        grid=(n_tok,),
        in_specs=[tok_spec] + _ffn_weight_specs(),
        out_specs=tok_spec,
        out_shape=jax.ShapeDtypeStruct((T, D_MODEL), f32),
        compiler_params=params_1d,
        name="ffn1",
    )(x.reshape(T, D_MODEL), row(ffn1_norm), ffn1_w_gate.astype(bf16), ffn1_w_up.astype(bf16),
      ffn1_w_down.astype(bf16))

    c0 = POOL_WIDTH
    w_pv = w_in[:, :c0].astype(bf16)
    w_k = w_in[:, c0 + ATTN_WIDTH:c0 + 2 * ATTN_WIDTH].astype(bf16)
    w_t = jnp.concatenate([w_in[:, c0:c0 + ATTN_WIDTH], w_in[:, c0 + 2 * ATTN_WIDTH:],
                           jnp.zeros((D_MODEL, F_ROWS - N_HEADS), w_in.dtype)], axis=1).T.astype(bf16)
    b_ft = jnp.pad(b_forget.astype(f32), (0, F_ROWS - N_HEADS)).reshape(F_ROWS, 1)
    q_gain = (jnp.tile(q_norm, N_HEADS).astype(f32) * (LOG2E / math.sqrt(HEAD_DIM))).reshape(ATTN_WIDTH, 1)
    k_gain = row(jnp.tile(k_norm, N_HEADS))
    head_id = jnp.arange(ATTN_WIDTH) // HEAD_DIM
    headsum = (head_id[:, None] == head_id[None, :]).astype(bf16)
    triu = (jnp.arange(TOKEN_TILE)[:, None] <= jnp.arange(TOKEN_TILE)[None, :]).astype(bf16)
    selk, ck = _k_aug_constants()

    seq_spec = lambda width: pl.BlockSpec((1, TOKEN_TILE, width), lambda b, s: (b, s, 0))
    y_pool, qt, qaugt, ka, vt = pl.pallas_call(
        _in_proj_kernel,
        grid=(B, S // TOKEN_TILE),
        in_specs=[seq_spec(D_MODEL), _resident((1, D_MODEL)),
                  _resident((D_MODEL, POOL_WIDTH)), _resident((D_MODEL, ATTN_WIDTH)),
                  _resident((2 * ATTN_WIDTH + F_ROWS, D_MODEL)), _resident((F_ROWS, 1)),
                  _resident((POOL_GROUPS, POOL_GROUP_DIM, POOL_GROUP_DIM)), _resident((1, POOL_WIDTH)),
                  _resident((1, POOL_WIDTH)), _resident((ATTN_WIDTH, 1)), _resident((1, ATTN_WIDTH)),
                  _resident((ATTN_WIDTH, ATTN_WIDTH)), _resident((TOKEN_TILE, TOKEN_TILE)),
                  _resident((LANES, N_HEAD_BLOCKS * LANES)), _resident((1, N_HEAD_BLOCKS * LANES))],
        out_specs=[seq_spec(POOL_WIDTH),
                   pl.BlockSpec((1, ATTN_WIDTH, TOKEN_TILE), lambda b, s: (b, 0, s)),
                   pl.BlockSpec((1, N_HEADS, BF16_ROWS, TOKEN_TILE), lambda b, s: (b, 0, 0, s)),
                   seq_spec(N_HEAD_BLOCKS * QK_DEPTH),
                   pl.BlockSpec((1, ATTN_WIDTH, TOKEN_TILE), lambda b, s: (b, 0, s))],
        out_shape=[jax.ShapeDtypeStruct((B, S, POOL_WIDTH), bf16),
                   jax.ShapeDtypeStruct((B, ATTN_WIDTH, S), bf16),
                   jax.ShapeDtypeStruct((B, N_HEADS, BF16_ROWS, S), bf16),
                   jax.ShapeDtypeStruct((B, S, N_HEAD_BLOCKS * QK_DEPTH), bf16),
                   jax.ShapeDtypeStruct((B, ATTN_WIDTH, S), bf16)],
        scratch_shapes=[pltpu.VMEM((POOL_HALO, POOL_WIDTH), f32), pltpu.VMEM((F_ROWS, LANES), f32)],
        compiler_params=params_2d,
        name="in_proj",
    )(x1.reshape(B, S, D_MODEL), row(mix_norm), w_pv, w_k, w_t, b_ft,
      pool_w.astype(bf16), row(pool_scale), row(out_norm_pool), q_gain, k_gain, headsum, triu, selk, ck)

    o_attn = pl.pallas_call(
        _attn_kernel,
        grid=(B, S // ATTN_TILE),
        in_specs=[pl.BlockSpec((1, ATTN_WIDTH, ATTN_TILE), lambda b, i: (b, 0, i)),
                  pl.BlockSpec((1, N_HEADS, BF16_ROWS, ATTN_TILE), lambda b, i: (b, 0, 0, i)),
                  pl.BlockSpec((1, S, N_HEAD_BLOCKS * QK_DEPTH), lambda b, i: (b, 0, 0)),
                  pl.BlockSpec((1, ATTN_WIDTH, S), lambda b, i: (b, 0, 0))],
        out_specs=pl.BlockSpec((1, ATTN_TILE, ATTN_WIDTH), lambda b, i: (b, i, 0)),
        out_shape=jax.ShapeDtypeStruct((B, S, ATTN_WIDTH), f32),
        scratch_shapes=[pltpu.VMEM((N_HEADS, LANES, S), bf16),
                        pltpu.VMEM((N_HEADS, QK_DEPTH, ATTN_TILE), bf16),
                        pltpu.VMEM((2, N_HEADS, ATTN_TILE, ATTN_TILE), f32),
                        pltpu.VMEM((N_HEADS, 1, ATTN_TILE), f32),
                        pltpu.VMEM((N_HEADS, LANES, ATTN_TILE), f32)],
        compiler_params=params_2d,
        name="attention",
    )(qt, qaugt, ka, vt)

    half_spec = pl.BlockSpec((TOKEN_TILE, POOL_WIDTH), lambda t: (t, 0))
    out = pl.pallas_call(
        _mix_ffn2_kernel,
        grid=(n_tok,),
        in_specs=[tok_spec, half_spec, half_spec, _resident((1, ATTN_WIDTH)),
                  _resident((D_MODEL, D_MODEL))] + _ffn_weight_specs(),
        out_specs=tok_spec,
        out_shape=jax.ShapeDtypeStruct((T, D_MODEL), f32),
        compiler_params=params_1d,
        name="mix_ffn2",
    )(x1, y_pool.reshape(T, POOL_WIDTH), o_attn.reshape(T, ATTN_WIDTH), row(out_norm_attn),
      w_out.astype(bf16), row(ffn2_norm), ffn2_w_gate.astype(bf16), ffn2_w_up.astype(bf16),
      ffn2_w_down.astype(bf16))
    return out.reshape(B, S, D_MODEL)
```

```python
import math

import numpy as np
import jax
import jax.numpy as jnp
from jax import lax
from jax.experimental import pallas as pl
from jax.experimental.pallas import tpu as pltpu

D_MODEL = 1024
D_FF = 2816
POOL_WIDTH = 512
POOL_GROUPS = 4
POOL_GROUP_DIM = 128
POOL_WINDOWS = (2, 4, 8, 16)
ATTN_WIDTH = 512
HEAD_DIM = 64
N_HEADS = 8
EPS = 1e-6
LOG2E = math.log2(math.e)

LANES = 128
BF16_ROWS = 16
F_ROWS = BF16_ROWS
HEADS_PER_BLOCK = LANES // HEAD_DIM
N_HEAD_BLOCKS = N_HEADS // HEADS_PER_BLOCK
POOL_HALO = 16
N_SPLIT = 3
AUG_STRIDE = 2 * N_SPLIT
QK_DEPTH = 2 * LANES

TOKEN_TILE = 512
FF_CHUNK = 256
ATTN_TILE = 256
VMEM_LIMIT = 56 * 1024 * 1024
MASK_VALUE = -1e30

bf16 = jnp.bfloat16
f32 = jnp.float32


def _rms_scale(x):
    return lax.rsqrt(jnp.mean(x * x, axis=-1, keepdims=True) + EPS)


def _swiglu_residual(x, norm_w, wg_ref, wu_ref, wd_ref):
    h = (x * norm_w).astype(bf16)
    r = _rms_scale(x)
    acc = jnp.zeros(x.shape, f32)
    for c in range(D_FF // FF_CHUNK):
        cols = slice(c * FF_CHUNK, (c + 1) * FF_CHUNK)
        g = r * jnp.dot(h, wg_ref[:, cols], preferred_element_type=f32)
        u = jnp.dot(h, wu_ref[:, cols], preferred_element_type=f32)
        a = (g * jax.nn.sigmoid(g) * u).astype(bf16)
        acc = acc + jnp.dot(a, wd_ref[cols, :], preferred_element_type=f32)
    return x + (0.5 * r) * acc


def _ffn1_kernel(x_ref, nw_ref, wg_ref, wu_ref, wd_ref, g2_ref, u2_ref, d2_ref, wo_ref,
                 o_ref, g2_bf_ref, u2_bf_ref, d2_bf_ref, wo_bf_ref):
    for src_ref, dst_ref in ((g2_ref, g2_bf_ref), (u2_ref, u2_bf_ref), (d2_ref, d2_bf_ref), (wo_ref, wo_bf_ref)):
        dst_ref[...] = src_ref[...].astype(bf16)
    o_ref[...] = _swiglu_residual(x_ref[...], nw_ref[...], wg_ref, wu_ref, wd_ref)


def _mix_ffn2_kernel(x_ref, yp_ref, oa_first_ref, oa_second_ref, nattn_ref, wo_ref, nw_ref, wg_ref, wu_ref, wd_ref,
                     o_ref):
    oa = jnp.concatenate([oa_first_ref[0, 0, 0], oa_second_ref[0, 0, 0]], axis=0)
    ya = (oa * _rms_scale(oa) * nattn_ref[...]).astype(bf16)
    x2 = (x_ref[...]
          + jnp.dot(yp_ref[...], wo_ref[:POOL_WIDTH, :], preferred_element_type=f32)
          + jnp.dot(ya, wo_ref[POOL_WIDTH:, :], preferred_element_type=f32))
    o_ref[...] = _swiglu_residual(x2, nw_ref[...], wg_ref, wu_ref, wd_ref)


def _split3(x):
    hi = x.astype(bf16).astype(f32)
    r = x - hi
    mid = r.astype(bf16).astype(f32)
    lo = (r - mid).astype(bf16).astype(f32)
    return hi, mid, lo


def _in_proj_kernel(x_ref, nw_ref, wpv_ref, wk_ref, wt_ref, bft_ref,
                    poolw_ref, pscale_ref, npool_ref, qn_ref, kn_ref, headsum_ref, triu_ref,
                    selk_ref, ck_ref,
                    yp_ref, qt_ref, qaugt_ref, ka_ref, vt_ref,
                    halo_ref, carry_ref):
    s_idx = pl.program_id(1)
    tm = x_ref.shape[1]

    @pl.when(s_idx == 0)
    def _():
        halo_ref[...] = jnp.zeros(halo_ref.shape, f32)
        carry_ref[...] = jnp.zeros(carry_ref.shape, f32)

    x = x_ref[0]
    h = (x * nw_ref[...]).astype(bf16)
    r = _rms_scale(x)
    r_row = jnp.broadcast_to(r, (tm, LANES)).T[0:1]

    qvf = r_row * lax.dot_general(wt_ref[...], h, (((1,), (1,)), ((), ())), preferred_element_type=f32)
    pv = r * jnp.dot(h, wpv_ref[...], preferred_element_type=f32)
    t = r * jnp.dot(h, wk_ref[...], preferred_element_type=f32)

    fl = qvf[2 * ATTN_WIDTH:] + bft_ref[...]
    logf = (jnp.minimum(fl, 0.0) - jnp.log1p(jnp.exp(-jnp.abs(fl)))) * LOG2E
    logf = jnp.where(lax.broadcasted_iota(jnp.int32, logf.shape, 0) < N_HEADS, logf, 0.0)
    parts = jnp.concatenate(_split3(logf), axis=0).astype(bf16)
    sums = jnp.dot(parts, triu_ref[...], preferred_element_type=f32)
    fcum_t = (sums[:F_ROWS] + sums[F_ROWS:2 * F_ROWS] + sums[2 * F_ROWS:]) + carry_ref[:, 0:1]
    carry_ref[...] = jnp.broadcast_to(fcum_t[:, tm - 1:tm], carry_ref.shape)
    hi_t, mid_t, lo_t = (part[:N_HEADS] for part in _split3(fcum_t))

    full = jnp.concatenate([halo_ref[...], pv], axis=0)
    halo_ref[...] = pv[tm - POOL_HALO:, :]
    pos = s_idx * tm + lax.broadcasted_iota(jnp.int32, (tm, 1), 0)
    mixed = []
    for g in range(POOL_GROUPS):
        w = POOL_WINDOWS[g]
        cols = slice(g * POOL_GROUP_DIM, (g + 1) * POOL_GROUP_DIM)
        win = full[:, cols]
        shift = 1
        while shift < w:
            win = win + pltpu.roll(win, shift, 0)
            shift *= 2
        count = jnp.minimum(pos + 1, w).astype(f32)
        pooled = win[POOL_HALO:, :] / count - pv[:, cols]
        mixed.append(jnp.dot(pooled.astype(bf16), poolw_ref[g], preferred_element_type=f32))
    mixed = jnp.concatenate(mixed, axis=1) * pscale_ref[...]
    yp_ref[0] = (mixed * _rms_scale(mixed) * npool_ref[...]).astype(bf16)

    ssq = jnp.dot((t * t).astype(bf16), headsum_ref[...], preferred_element_type=f32)
    kn = (t * lax.rsqrt(ssq * (1.0 / HEAD_DIM) + EPS) * kn_ref[...]).astype(bf16)

    packed_t = jnp.concatenate([hi_t, mid_t, lo_t, jnp.zeros((LANES - N_SPLIT * N_HEADS, tm), f32)], axis=0)
    packed = packed_t.T.astype(bf16)
    kaug = (jnp.dot(packed, selk_ref[...], preferred_element_type=f32) + ck_ref[...]).astype(bf16)
    pieces = []
    for p in range(N_HEAD_BLOCKS):
        cols = slice(p * LANES, (p + 1) * LANES)
        pieces += [kn[:, cols], kaug[:, cols]]
    ka_ref[0] = jnp.concatenate(pieces, axis=1)

    qt = qvf[:ATTN_WIDTH].reshape(N_HEADS, HEAD_DIM, tm)
    qt = qt * lax.rsqrt(jnp.mean(qt * qt, axis=1, keepdims=True) + EPS)
    qt_ref[0] = (qt.reshape(ATTN_WIDTH, tm) * qn_ref[...]).astype(bf16)
    vt_ref[0] = qvf[ATTN_WIDTH:2 * ATTN_WIDTH].astype(bf16)

    rowid = lax.broadcasted_iota(jnp.int32, (BF16_ROWS, tm), 0)
    for hh in range(N_HEADS):
        base = (hh % HEADS_PER_BLOCK) * AUG_STRIDE
        ones_rows = ((rowid >= base + N_SPLIT) & (rowid < base + 2 * N_SPLIT)).astype(f32)
        aug = jnp.where(rowid == base, hi_t[hh:hh + 1],
                        jnp.where(rowid == base + 1, mid_t[hh:hh + 1],
                                  jnp.where(rowid == base + 2, lo_t[hh:hh + 1], ones_rows)))
        qaugt_ref[0, hh] = aug.astype(bf16)


def _attn_kernel(qt_lo_ref, qt_hi_ref, qaugt_lo_ref, qaugt_hi_ref, ka_ref, vt_ref, o_ref,
                 vth_ref, qat_ref, s_ref, m_ref, acc_ref):
    i = pl.program_id(1)
    tq = o_ref.shape[3]
    tk = tq
    n_tiles = ka_ref.shape[1] // tk
    qt_refs = (qt_lo_ref, qt_hi_ref)
    qaugt_refs = (qaugt_lo_ref, qaugt_hi_ref)

    def build_values_operand(hh):
        seq = vt_ref.shape[2]
        ones_then_zeros = (lax.broadcasted_iota(jnp.int32, (HEAD_DIM, seq), 0) == 0).astype(bf16)
        vh = vt_ref[0, hh * HEAD_DIM:(hh + 1) * HEAD_DIM, :]
        odd = hh % HEADS_PER_BLOCK
        vth_ref[hh, 0:HEAD_DIM, :] = ones_then_zeros if odd else vh
        vth_ref[hh, HEAD_DIM:LANES, :] = vh if odd else ones_then_zeros

    def build_query_operand(w, hh):
        zeros_head = jnp.zeros((HEAD_DIM, tq), bf16)
        qh = qt_refs[w][0, hh * HEAD_DIM:(hh + 1) * HEAD_DIM, :]
        odd = hh % HEADS_PER_BLOCK
        qat_ref[w, hh, 0:HEAD_DIM, :] = zeros_head if odd else qh
        qat_ref[w, hh, HEAD_DIM:LANES, :] = qh if odd else zeros_head
        qat_ref[w, hh, LANES:LANES + BF16_ROWS, :] = qaugt_refs[w][0, hh]
        qat_ref[w, hh, LANES + BF16_ROWS:, :] = jnp.zeros((QK_DEPTH - LANES - BF16_ROWS, tq), bf16)

    def keys(j):
        return slice(j * tk, (j + 1) * tk)

    def logits(w, j, hh, masked):
        p = hh // HEADS_PER_BLOCK
        ka = ka_ref[0, keys(j), p * QK_DEPTH:(p + 1) * QK_DEPTH]
        st = jnp.dot(ka, qat_ref[w, hh], preferred_element_type=f32)
        if masked:
            key_minus_query = (lax.broadcasted_iota(jnp.int32, (tk, tq), 0)
                               - lax.broadcasted_iota(jnp.int32, (tk, tq), 1))
            st = jnp.where(key_minus_query > 0, MASK_VALUE, st)
        s_ref[w, j % 2, hh] = st

    def softmax_values(w, j, hh):
        st_ref = s_ref.at[w, j % 2, hh]
        vt = vth_ref[hh, :, keys(j)]
        m_blk = jnp.max(st_ref[...], axis=0, keepdims=True)
        if j == 0:
            m_ref[w, hh] = m_blk
            pt = jnp.exp2(st_ref[...] - m_blk).astype(bf16)
            acc_ref[w, hh] = jnp.dot(vt, pt, preferred_element_type=f32)
        else:
            m_prev = m_ref[w, hh]
            m_new = jnp.maximum(m_prev, m_blk)
            m_ref[w, hh] = m_new
            pt = jnp.exp2(st_ref[...] - m_new).astype(bf16)
            acc_ref[w, hh] = (jnp.exp2(m_prev - m_new) * acc_ref[w, hh]
                              + jnp.dot(vt, pt, preferred_element_type=f32))

    def write_output(w, p):
        row = lax.broadcasted_iota(jnp.int32, (LANES, tq), 0)
        even, odd = acc_ref[w, 2 * p], acc_ref[w, 2 * p + 1]
        ot = jnp.where(row < HEAD_DIM, even / even[HEAD_DIM:HEAD_DIM + 1, :], odd / odd[0:1, :])
        o_ref[0, w, 0, :, p * LANES:(p + 1) * LANES] = ot.T

    for i_static in range(n_tiles // 2):
        @pl.when(i == i_static)
        def _(n_blocks=(i_static + 1, n_tiles - i_static)):
            def start(w):
                for hh in range(N_HEADS):
                    build_query_operand(w, hh)
                    logits(w, 0, hh, masked=(n_blocks[w] == 1))

            def key_block(w, j):
                for hh in range(N_HEADS):
                    if i_static == 0 and j == 0 and w == 1:
                        build_values_operand(hh)
                    softmax_values(w, j, hh)
                    if j + 1 < n_blocks[w]:
                        logits(w, j + 1, hh, masked=(j + 2 == n_blocks[w]))
                    elif hh % HEADS_PER_BLOCK == HEADS_PER_BLOCK - 1:
                        write_output(w, hh // HEADS_PER_BLOCK)

            start(1)
            for j in range(n_blocks[1] - 1):
                key_block(1, j)
            start(0)
            key_block(1, n_blocks[1] - 1)
            for j in range(n_blocks[0]):
                key_block(0, j)


def _resident(shape):
    nd = len(shape)
    return pl.BlockSpec(shape, lambda *_: (0,) * nd, pipeline_mode=pl.Buffered(1))


def _ffn_weight_specs():
    return [_resident((1, D_MODEL)), _resident((D_MODEL, D_FF)), _resident((D_MODEL, D_FF)),
            _resident((D_FF, D_MODEL))]


def _k_aug_constants():
    selk = np.zeros((LANES, N_HEAD_BLOCKS * LANES), np.float32)
    ck = np.zeros((1, N_HEAD_BLOCKS * LANES), np.float32)
    for hh in range(N_HEADS):
        p, odd = divmod(hh, HEADS_PER_BLOCK)
        base = odd * AUG_STRIDE
        for part in range(N_SPLIT):
            ck[0, p * LANES + base + part] = 1.0
            selk[part * N_HEADS + hh, p * LANES + base + N_SPLIT + part] = -1.0
    return jnp.asarray(selk, bf16), jnp.asarray(ck)


def kernel(x, ffn1_norm, ffn1_w_gate, ffn1_w_up, ffn1_w_down, mix_norm, w_in, b_forget, pool_w, pool_scale, q_norm, k_norm, out_norm_pool, out_norm_attn, w_out, ffn2_norm, ffn2_w_gate, ffn2_w_up, ffn2_w_down):
    B, S, D = x.shape
    assert D == D_MODEL and S % TOKEN_TILE == 0 and S % ATTN_TILE == 0
    T = B * S
    n_tok = T // TOKEN_TILE
    row = lambda a: a.reshape(1, -1).astype(f32)

    tok_spec = pl.BlockSpec((TOKEN_TILE, D_MODEL), lambda t: (t, 0))
    params_1d = pltpu.CompilerParams(dimension_semantics=("arbitrary",), vmem_limit_bytes=VMEM_LIMIT)
    params_2d = pltpu.CompilerParams(dimension_semantics=("arbitrary", "arbitrary"), vmem_limit_bytes=VMEM_LIMIT)

    assert D_MODEL % n_tok == 0 and (D_MODEL // n_tok) % BF16_ROWS == 0
    slab = D_MODEL // n_tok
    ff_slab = pl.BlockSpec((slab, D_FF), lambda t: (t, 0))
    wo_slab = pl.BlockSpec((slab, D_MODEL), lambda t: (t, 0))
    x1, w_gate2, w_up2, w_down2, w_out_bf = pl.pallas_call(
        _ffn1_kernel,
        grid=(n_tok,),
        in_specs=[tok_spec] + _ffn_weight_specs() + [ff_slab, ff_slab, ff_slab, wo_slab],
        out_specs=[tok_spec, ff_slab, ff_slab, ff_slab, wo_slab],
        out_shape=[jax.ShapeDtypeStruct((T, D_MODEL), f32)]
                  + [jax.ShapeDtypeStruct((D_MODEL, D_FF), bf16)] * 3
                  + [jax.ShapeDtypeStruct((D_MODEL, D_MODEL), bf16)],
        compiler_params=params_1d,
        name="ffn1",
    )(x.reshape(T, D_MODEL), row(ffn1_norm), ffn1_w_gate.astype(bf16), ffn1_w_up.astype(bf16),
      ffn1_w_down.astype(bf16), ffn2_w_gate, ffn2_w_up, ffn2_w_down.reshape(D_MODEL, D_FF), w_out)
    w_down2 = w_down2.reshape(D_FF, D_MODEL)

    c0 = POOL_WIDTH
    w_pv = w_in[:, :c0].astype(bf16)
    w_k = w_in[:, c0 + ATTN_WIDTH:c0 + 2 * ATTN_WIDTH].astype(bf16)
    w_t = jnp.concatenate([w_in[:, c0:c0 + ATTN_WIDTH], w_in[:, c0 + 2 * ATTN_WIDTH:],
                           jnp.zeros((D_MODEL, F_ROWS - N_HEADS), w_in.dtype)], axis=1).T.astype(bf16)
    b_ft = jnp.pad(b_forget.astype(f32), (0, F_ROWS - N_HEADS)).reshape(F_ROWS, 1)
    q_gain = (jnp.tile(q_norm, N_HEADS).astype(f32) * (LOG2E / math.sqrt(HEAD_DIM))).reshape(ATTN_WIDTH, 1)
    k_gain = row(jnp.tile(k_norm, N_HEADS))
    head_id = jnp.arange(ATTN_WIDTH) // HEAD_DIM
    headsum = (head_id[:, None] == head_id[None, :]).astype(bf16)
    triu = (jnp.arange(TOKEN_TILE)[:, None] <= jnp.arange(TOKEN_TILE)[None, :]).astype(bf16)
    selk, ck = _k_aug_constants()

    seq_spec = lambda width: pl.BlockSpec((1, TOKEN_TILE, width), lambda b, s: (b, s, 0))
    y_pool, qt, qaugt, ka, vt = pl.pallas_call(
        _in_proj_kernel,
        grid=(B, S // TOKEN_TILE),
        in_specs=[seq_spec(D_MODEL), _resident((1, D_MODEL)),
                  _resident((D_MODEL, POOL_WIDTH)), _resident((D_MODEL, ATTN_WIDTH)),
                  _resident((2 * ATTN_WIDTH + F_ROWS, D_MODEL)), _resident((F_ROWS, 1)),
                  _resident((POOL_GROUPS, POOL_GROUP_DIM, POOL_GROUP_DIM)), _resident((1, POOL_WIDTH)),
                  _resident((1, POOL_WIDTH)), _resident((ATTN_WIDTH, 1)), _resident((1, ATTN_WIDTH)),
                  _resident((ATTN_WIDTH, ATTN_WIDTH)), _resident((TOKEN_TILE, TOKEN_TILE)),
                  _resident((LANES, N_HEAD_BLOCKS * LANES)), _resident((1, N_HEAD_BLOCKS * LANES))],
        out_specs=[seq_spec(POOL_WIDTH),
                   pl.BlockSpec((1, ATTN_WIDTH, TOKEN_TILE), lambda b, s: (b, 0, s)),
                   pl.BlockSpec((1, N_HEADS, BF16_ROWS, TOKEN_TILE), lambda b, s: (b, 0, 0, s)),
                   seq_spec(N_HEAD_BLOCKS * QK_DEPTH),
                   pl.BlockSpec((1, ATTN_WIDTH, TOKEN_TILE), lambda b, s: (b, 0, s))],
        out_shape=[jax.ShapeDtypeStruct((B, S, POOL_WIDTH), bf16),
                   jax.ShapeDtypeStruct((B, ATTN_WIDTH, S), bf16),
                   jax.ShapeDtypeStruct((B, N_HEADS, BF16_ROWS, S), bf16),
                   jax.ShapeDtypeStruct((B, S, N_HEAD_BLOCKS * QK_DEPTH), bf16),
                   jax.ShapeDtypeStruct((B, ATTN_WIDTH, S), bf16)],
        scratch_shapes=[pltpu.VMEM((POOL_HALO, POOL_WIDTH), f32), pltpu.VMEM((F_ROWS, LANES), f32)],
        compiler_params=params_2d,
        name="in_proj",
    )(x1.reshape(B, S, D_MODEL), row(mix_norm), w_pv, w_k, w_t, b_ft,
      pool_w.astype(bf16), row(pool_scale), row(out_norm_pool), q_gain, k_gain, headsum, triu, selk, ck)

    n_q = S // ATTN_TILE
    assert n_q % 2 == 0 and TOKEN_TILE == 2 * ATTN_TILE
    o_attn = pl.pallas_call(
        _attn_kernel,
        grid=(B, n_q // 2),
        in_specs=[pl.BlockSpec((1, ATTN_WIDTH, ATTN_TILE), lambda b, i: (b, 0, i)),
                  pl.BlockSpec((1, ATTN_WIDTH, ATTN_TILE), lambda b, i: (b, 0, n_q - 1 - i)),
                  pl.BlockSpec((1, N_HEADS, BF16_ROWS, ATTN_TILE), lambda b, i: (b, 0, 0, i)),
                  pl.BlockSpec((1, N_HEADS, BF16_ROWS, ATTN_TILE), lambda b, i: (b, 0, 0, n_q - 1 - i)),
                  pl.BlockSpec((1, S, N_HEAD_BLOCKS * QK_DEPTH), lambda b, i: (b, 0, 0)),
                  pl.BlockSpec((1, ATTN_WIDTH, S), lambda b, i: (b, 0, 0))],
        out_specs=pl.BlockSpec((1, 2, 1, ATTN_TILE, ATTN_WIDTH), lambda b, i: (b, 0, i, 0, 0)),
        out_shape=jax.ShapeDtypeStruct((B, 2, n_q // 2, ATTN_TILE, ATTN_WIDTH), f32),
        scratch_shapes=[pltpu.VMEM((N_HEADS, LANES, S), bf16),
                        pltpu.VMEM((2, N_HEADS, QK_DEPTH, ATTN_TILE), bf16),
                        pltpu.VMEM((2, 2, N_HEADS, ATTN_TILE, ATTN_TILE), f32),
                        pltpu.VMEM((2, N_HEADS, 1, ATTN_TILE), f32),
                        pltpu.VMEM((2, N_HEADS, LANES, ATTN_TILE), f32)],
        compiler_params=params_2d,
        name="attention",
    )(qt, qt, qaugt, qaugt, ka, vt)

    def attn_tile_spec(which):
        def index(t):
            q = (t % (S // TOKEN_TILE)) * 2 + which
            high = q // (n_q // 2)
            return (t // (S // TOKEN_TILE), high, jnp.where(high == 0, q, n_q - 1 - q), 0, 0)
        return pl.BlockSpec((1, 1, 1, ATTN_TILE, ATTN_WIDTH), index)

    half_spec = pl.BlockSpec((TOKEN_TILE, POOL_WIDTH), lambda t: (t, 0))
    out = pl.pallas_call(
        _mix_ffn2_kernel,
        grid=(n_tok,),
        in_specs=[tok_spec, half_spec, attn_tile_spec(0), attn_tile_spec(1), _resident((1, ATTN_WIDTH)),
                  _resident((D_MODEL, D_MODEL))] + _ffn_weight_specs(),
        out_specs=tok_spec,
        out_shape=jax.ShapeDtypeStruct((T, D_MODEL), f32),
        compiler_params=params_1d,
        name="mix_ffn2",
    )(x1, y_pool.reshape(T, POOL_WIDTH), o_attn, o_attn, row(out_norm_attn),
      w_out_bf, row(ffn2_norm), w_gate2, w_up2, w_down2)
    return out.reshape(B, S, D_MODEL)
```

```python
import math

import numpy as np
import jax
import jax.numpy as jnp
from jax import lax
from jax.experimental import pallas as pl
from jax.experimental.pallas import tpu as pltpu

D_MODEL = 1024
D_FF = 2816
POOL_WIDTH = 512
POOL_GROUPS = 4
POOL_GROUP_DIM = 128
POOL_WINDOWS = (2, 4, 8, 16)
ATTN_WIDTH = 512
HEAD_DIM = 64
N_HEADS = 8
EPS = 1e-6
LOG2E = math.log2(math.e)

LANES = 128
BF16_ROWS = 16
F_ROWS = BF16_ROWS
HEADS_PER_BLOCK = LANES // HEAD_DIM
N_HEAD_BLOCKS = N_HEADS // HEADS_PER_BLOCK
POOL_HALO = 16
N_SPLIT = 3
AUG_STRIDE = 2 * N_SPLIT
QK_DEPTH = 2 * LANES

TOKEN_TILE = 512
FFN_PASSES = 2
FF_CHUNK = 256
ATTN_TILE = 256
VMEM_LIMIT = 56 * 1024 * 1024
MASK_VALUE = -1e30

bf16 = jnp.bfloat16
f32 = jnp.float32


def _rms_scale(x):
    return lax.rsqrt(jnp.mean(x * x, axis=-1, keepdims=True) + EPS)


def _swiglu_residual(x, norm_w, wg_ref, wu_ref, wd_ref):
    h = (x * norm_w).astype(bf16)
    r = _rms_scale(x)
    acc = jnp.zeros(x.shape, f32)
    for c in range(D_FF // FF_CHUNK):
        cols = slice(c * FF_CHUNK, (c + 1) * FF_CHUNK)
        g = r * jnp.dot(h, wg_ref[:, cols], preferred_element_type=f32)
        u = jnp.dot(h, wu_ref[:, cols], preferred_element_type=f32)
        a = (g * jax.nn.sigmoid(g) * u).astype(bf16)
        acc = acc + jnp.dot(a, wd_ref[cols, :], preferred_element_type=f32)
    return x + (0.5 * r) * acc


def _ffn1_kernel(x_ref, nw_ref, wg_ref, wu_ref, wd_ref, g2_ref, u2_ref, d2_ref, wo_ref,
                 o_ref, g2_bf_ref, u2_bf_ref, d2_bf_ref, wo_bf_ref):
    for src_ref, dst_ref in ((g2_ref, g2_bf_ref), (u2_ref, u2_bf_ref), (wo_ref, wo_bf_ref)):
        dst_ref[...] = src_ref[...].astype(bf16)

    @pl.when(pl.program_id(0) < D_FF // d2_ref.shape[0])
    def _():
        d2_bf_ref[...] = d2_ref[...].astype(bf16)

    for s in range(FFN_PASSES):
        rows = slice(s * TOKEN_TILE, (s + 1) * TOKEN_TILE)
        o_ref[rows, :] = _swiglu_residual(x_ref[rows, :], nw_ref[...], wg_ref, wu_ref, wd_ref)


def _mix_ffn2_kernel(x_ref, yp_ref, *rest):
    tiles_per_pass = TOKEN_TILE // ATTN_TILE
    oa_refs = rest[:FFN_PASSES * tiles_per_pass]
    nattn_ref, wo_ref, nw_ref, wg_ref, wu_ref, wd_ref, o_ref = rest[FFN_PASSES * tiles_per_pass:]
    for s in range(FFN_PASSES):
        rows = slice(s * TOKEN_TILE, (s + 1) * TOKEN_TILE)
        oa = jnp.concatenate([ref[0, 0, 0] for ref in oa_refs[s * tiles_per_pass:(s + 1) * tiles_per_pass]], axis=0)
        ya = (oa * _rms_scale(oa) * nattn_ref[...]).astype(bf16)
        x2 = (x_ref[rows, :]
              + jnp.dot(yp_ref[rows, :], wo_ref[:POOL_WIDTH, :], preferred_element_type=f32)
              + jnp.dot(ya, wo_ref[POOL_WIDTH:, :], preferred_element_type=f32))
        o_ref[rows, :] = _swiglu_residual(x2, nw_ref[...], wg_ref, wu_ref, wd_ref)


def _split3(x):
    hi = x.astype(bf16).astype(f32)
    r = x - hi
    mid = r.astype(bf16).astype(f32)
    lo = (r - mid).astype(bf16).astype(f32)
    return hi, mid, lo


def _in_proj_kernel(x_ref, nw_ref, wpv_ref, wk_ref, wt_ref, bft_ref,
                    poolw_ref, pscale_ref, npool_ref, qn_ref, kn_ref, headsum_ref, triu_ref,
                    selk_ref, ck_ref,
                    yp_ref, qt_ref, qaugt_ref, ka_ref, vt_ref,
                    halo_ref, carry_ref):
    s_idx = pl.program_id(1)
    tm = x_ref.shape[1]

    @pl.when(s_idx == 0)
    def _():
        halo_ref[...] = jnp.zeros(halo_ref.shape, f32)
        carry_ref[...] = jnp.zeros(carry_ref.shape, f32)

    x = x_ref[0]
    h = (x * nw_ref[...]).astype(bf16)
    r = _rms_scale(x)
    r_row = jnp.broadcast_to(r, (tm, LANES)).T[0:1]

    qvf = r_row * lax.dot_general(wt_ref[...], h, (((1,), (1,)), ((), ())), preferred_element_type=f32)
    pv = r * jnp.dot(h, wpv_ref[...], preferred_element_type=f32)
    t = r * jnp.dot(h, wk_ref[...], preferred_element_type=f32)

    fl = qvf[2 * ATTN_WIDTH:] + bft_ref[...]
    logf = (jnp.minimum(fl, 0.0) - jnp.log1p(jnp.exp(-jnp.abs(fl)))) * LOG2E
    logf = jnp.where(lax.broadcasted_iota(jnp.int32, logf.shape, 0) < N_HEADS, logf, 0.0)
    parts = jnp.concatenate(_split3(logf), axis=0).astype(bf16)
    sums = jnp.dot(parts, triu_ref[...], preferred_element_type=f32)
    fcum_t = (sums[:F_ROWS] + sums[F_ROWS:2 * F_ROWS] + sums[2 * F_ROWS:]) + carry_ref[:, 0:1]
    carry_ref[...] = jnp.broadcast_to(fcum_t[:, tm - 1:tm], carry_ref.shape)
    hi_t, mid_t, lo_t = (part[:N_HEADS] for part in _split3(fcum_t))

    full = jnp.concatenate([halo_ref[...], pv], axis=0)
    halo_ref[...] = pv[tm - POOL_HALO:, :]
    pos = s_idx * tm + lax.broadcasted_iota(jnp.int32, (tm, 1), 0)
    mixed = []
    for g in range(POOL_GROUPS):
        w = POOL_WINDOWS[g]
        cols = slice(g * POOL_GROUP_DIM, (g + 1) * POOL_GROUP_DIM)
        win = full[:, cols]
        shift = 1
        while shift < w:
            win = win + pltpu.roll(win, shift, 0)
            shift *= 2
        count = jnp.minimum(pos + 1, w).astype(f32)
        pooled = win[POOL_HALO:, :] / count - pv[:, cols]
        mixed.append(jnp.dot(pooled.astype(bf16), poolw_ref[g], preferred_element_type=f32))
    mixed = jnp.concatenate(mixed, axis=1) * pscale_ref[...]
    yp_ref[0] = (mixed * _rms_scale(mixed) * npool_ref[...]).astype(bf16)

    ssq = jnp.dot((t * t).astype(bf16), headsum_ref[...], preferred_element_type=f32)
    kn = (t * lax.rsqrt(ssq * (1.0 / HEAD_DIM) + EPS) * kn_ref[...]).astype(bf16)

    packed_t = jnp.concatenate([hi_t, mid_t, lo_t, jnp.zeros((LANES - N_SPLIT * N_HEADS, tm), f32)], axis=0)
    packed = packed_t.T.astype(bf16)
    kaug = (jnp.dot(packed, selk_ref[...], preferred_element_type=f32) + ck_ref[...]).astype(bf16)
    pieces = []
    for p in range(N_HEAD_BLOCKS):
        cols = slice(p * LANES, (p + 1) * LANES)
        pieces += [kn[:, cols], kaug[:, cols]]
    ka_ref[0] = jnp.concatenate(pieces, axis=1)

    qt = qvf[:ATTN_WIDTH].reshape(N_HEADS, HEAD_DIM, tm)
    qt = qt * lax.rsqrt(jnp.mean(qt * qt, axis=1, keepdims=True) + EPS)
    qt_ref[0] = (qt.reshape(ATTN_WIDTH, tm) * qn_ref[...]).astype(bf16)
    vt_ref[0] = qvf[ATTN_WIDTH:2 * ATTN_WIDTH].astype(bf16)

    rowid = lax.broadcasted_iota(jnp.int32, (BF16_ROWS, tm), 0)
    for hh in range(N_HEADS):
        base = (hh % HEADS_PER_BLOCK) * AUG_STRIDE
        ones_rows = ((rowid >= base + N_SPLIT) & (rowid < base + 2 * N_SPLIT)).astype(f32)
        aug = jnp.where(rowid == base, hi_t[hh:hh + 1],
                        jnp.where(rowid == base + 1, mid_t[hh:hh + 1],
                                  jnp.where(rowid == base + 2, lo_t[hh:hh + 1], ones_rows)))
        qaugt_ref[0, hh] = aug.astype(bf16)


def _attn_kernel(qt_lo_ref, qt_hi_ref, qaugt_lo_ref, qaugt_hi_ref, ka_ref, vt_ref, o_ref,
                 vth_ref, qat_ref, s_ref, m_ref, acc_ref):
    i = pl.program_id(1)
    tq = o_ref.shape[3]
    tk = tq
    n_tiles = ka_ref.shape[1] // tk
    qt_refs = (qt_lo_ref, qt_hi_ref)
    qaugt_refs = (qaugt_lo_ref, qaugt_hi_ref)

    def build_values_operand(hh):
        seq = vt_ref.shape[2]
        ones_then_zeros = (lax.broadcasted_iota(jnp.int32, (HEAD_DIM, seq), 0) == 0).astype(bf16)
        vh = vt_ref[0, hh * HEAD_DIM:(hh + 1) * HEAD_DIM, :]
        odd = hh % HEADS_PER_BLOCK
        vth_ref[hh, 0:HEAD_DIM, :] = ones_then_zeros if odd else vh
        vth_ref[hh, HEAD_DIM:LANES, :] = vh if odd else ones_then_zeros

    def build_query_operand(w, hh):
        zeros_head = jnp.zeros((HEAD_DIM, tq), bf16)
        qh = qt_refs[w][0, hh * HEAD_DIM:(hh + 1) * HEAD_DIM, :]
        odd = hh % HEADS_PER_BLOCK
        qat_ref[w, hh, 0:HEAD_DIM, :] = zeros_head if odd else qh
        qat_ref[w, hh, HEAD_DIM:LANES, :] = qh if odd else zeros_head
        qat_ref[w, hh, LANES:LANES + BF16_ROWS, :] = qaugt_refs[w][0, hh]
        qat_ref[w, hh, LANES + BF16_ROWS:, :] = jnp.zeros((QK_DEPTH - LANES - BF16_ROWS, tq), bf16)

    def keys(j):
        return slice(j * tk, (j + 1) * tk)

    def logits(w, j, hh, masked):
        p = hh // HEADS_PER_BLOCK
        ka = ka_ref[0, keys(j), p * QK_DEPTH:(p + 1) * QK_DEPTH]
        st = jnp.dot(ka, qat_ref[w, hh], preferred_element_type=f32)
        if masked:
            key_minus_query = (lax.broadcasted_iota(jnp.int32, (tk, tq), 0)
                               - lax.broadcasted_iota(jnp.int32, (tk, tq), 1))
            st = jnp.where(key_minus_query > 0, MASK_VALUE, st)
        s_ref[w, j % 2, hh] = st

    def softmax_values(w, j, hh):
        st_ref = s_ref.at[w, j % 2, hh]
        vt = vth_ref[hh, :, keys(j)]
        m_blk = jnp.max(st_ref[...], axis=0, keepdims=True)
        if j == 0:
            m_ref[w, hh] = m_blk
            pt = jnp.exp2(st_ref[...] - m_blk).astype(bf16)
            acc_ref[w, hh] = jnp.dot(vt, pt, preferred_element_type=f32)
        else:
            m_prev = m_ref[w, hh]
            m_new = jnp.maximum(m_prev, m_blk)
            m_ref[w, hh] = m_new
            pt = jnp.exp2(st_ref[...] - m_new).astype(bf16)
            acc_ref[w, hh] = (jnp.exp2(m_prev - m_new) * acc_ref[w, hh]
                              + jnp.dot(vt, pt, preferred_element_type=f32))

    def write_output(w, p):
        row = lax.broadcasted_iota(jnp.int32, (LANES, tq), 0)
        even, odd = acc_ref[w, 2 * p], acc_ref[w, 2 * p + 1]
        ot = jnp.where(row < HEAD_DIM, even / even[HEAD_DIM:HEAD_DIM + 1, :], odd / odd[0:1, :])
        o_ref[0, w, 0, :, p * LANES:(p + 1) * LANES] = ot.T

    for i_static in range(n_tiles // 2):
        @pl.when(i == i_static)
        def _(n_blocks=(i_static + 1, n_tiles - i_static)):
            def start(w):
                for hh in range(N_HEADS):
                    build_query_operand(w, hh)
                    logits(w, 0, hh, masked=(n_blocks[w] == 1))

            def key_block(w, j):
                for hh in range(N_HEADS):
                    if i_static == 0 and j == 0 and w == 1:
                        build_values_operand(hh)
                    softmax_values(w, j, hh)
                    if j + 1 < n_blocks[w]:
                        logits(w, j + 1, hh, masked=(j + 2 == n_blocks[w]))
                    elif hh % HEADS_PER_BLOCK == HEADS_PER_BLOCK - 1:
                        write_output(w, hh // HEADS_PER_BLOCK)

            start(1)
            for j in range(n_blocks[1] - 1):
                key_block(1, j)
            start(0)
            key_block(1, n_blocks[1] - 1)
            for j in range(n_blocks[0]):
                key_block(0, j)


def _resident(shape):
    nd = len(shape)
    return pl.BlockSpec(shape, lambda *_: (0,) * nd, pipeline_mode=pl.Buffered(1))


def _ffn_weight_specs():
    return [_resident((1, D_MODEL)), _resident((D_MODEL, D_FF)), _resident((D_MODEL, D_FF)),
            _resident((D_FF, D_MODEL))]


def _k_aug_constants():
    selk = np.zeros((LANES, N_HEAD_BLOCKS * LANES), np.float32)
    ck = np.zeros((1, N_HEAD_BLOCKS * LANES), np.float32)
    for hh in range(N_HEADS):
        p, odd = divmod(hh, HEADS_PER_BLOCK)
        base = odd * AUG_STRIDE
        for part in range(N_SPLIT):
            ck[0, p * LANES + base + part] = 1.0
            selk[part * N_HEADS + hh, p * LANES + base + N_SPLIT + part] = -1.0
    return jnp.asarray(selk, bf16), jnp.asarray(ck)


def kernel(x, ffn1_norm, ffn1_w_gate, ffn1_w_up, ffn1_w_down, mix_norm, w_in, b_forget, pool_w, pool_scale, q_norm, k_norm, out_norm_pool, out_norm_attn, w_out, ffn2_norm, ffn2_w_gate, ffn2_w_up, ffn2_w_down):
    B, S, D = x.shape
    ffn_tile = FFN_PASSES * TOKEN_TILE
    assert D == D_MODEL and S % ffn_tile == 0 and S % ATTN_TILE == 0
    T = B * S
    n_tok = T // ffn_tile
    row = lambda a: a.reshape(1, -1).astype(f32)

    tok_spec = pl.BlockSpec((ffn_tile, D_MODEL), lambda t: (t, 0))
    params_1d = pltpu.CompilerParams(dimension_semantics=("arbitrary",), vmem_limit_bytes=VMEM_LIMIT)
    params_2d = pltpu.CompilerParams(dimension_semantics=("arbitrary", "arbitrary"), vmem_limit_bytes=VMEM_LIMIT)

    assert D_MODEL % n_tok == 0 and (D_MODEL // n_tok) % BF16_ROWS == 0
    slab = D_MODEL // n_tok
    down_slab = BF16_ROWS * (D_FF // FF_CHUNK)
    n_down = D_FF // down_slab
    assert n_down * down_slab == D_FF and n_down <= n_tok
    ff_slab = pl.BlockSpec((slab, D_FF), lambda t: (t, 0))
    wd_slab = pl.BlockSpec((down_slab, D_MODEL), lambda t: (jnp.minimum(t, n_down - 1), 0))
    wo_slab = pl.BlockSpec((slab, D_MODEL), lambda t: (t, 0))
    x1, w_gate2, w_up2, w_down2, w_out_bf = pl.pallas_call(
        _ffn1_kernel,
        grid=(n_tok,),
        in_specs=[tok_spec] + _ffn_weight_specs() + [ff_slab, ff_slab, wd_slab, wo_slab],
        out_specs=[tok_spec, ff_slab, ff_slab, wd_slab, wo_slab],
        out_shape=[jax.ShapeDtypeStruct((T, D_MODEL), f32),
                   jax.ShapeDtypeStruct((D_MODEL, D_FF), bf16), jax.ShapeDtypeStruct((D_MODEL, D_FF), bf16),
                   jax.ShapeDtypeStruct((D_FF, D_MODEL), bf16), jax.ShapeDtypeStruct((D_MODEL, D_MODEL), bf16)],
        compiler_params=params_1d,
        name="ffn1",
    )(x.reshape(T, D_MODEL), row(ffn1_norm), ffn1_w_gate.astype(bf16), ffn1_w_up.astype(bf16),
      ffn1_w_down.astype(bf16), ffn2_w_gate, ffn2_w_up, ffn2_w_down, w_out)

    c0 = POOL_WIDTH
    w_pv = w_in[:, :c0].astype(bf16)
    w_k = w_in[:, c0 + ATTN_WIDTH:c0 + 2 * ATTN_WIDTH].astype(bf16)
    w_t = jnp.concatenate([w_in[:, c0:c0 + ATTN_WIDTH], w_in[:, c0 + 2 * ATTN_WIDTH:],
                           jnp.zeros((D_MODEL, F_ROWS - N_HEADS), w_in.dtype)], axis=1).T.astype(bf16)
    b_ft = jnp.pad(b_forget.astype(f32), (0, F_ROWS - N_HEADS)).reshape(F_ROWS, 1)
    q_gain = (jnp.tile(q_norm, N_HEADS).astype(f32) * (LOG2E / math.sqrt(HEAD_DIM))).reshape(ATTN_WIDTH, 1)
    k_gain = row(jnp.tile(k_norm, N_HEADS))
    head_id = jnp.arange(ATTN_WIDTH) // HEAD_DIM
    headsum = (head_id[:, None] == head_id[None, :]).astype(bf16)
    triu = (jnp.arange(TOKEN_TILE)[:, None] <= jnp.arange(TOKEN_TILE)[None, :]).astype(bf16)
    selk, ck = _k_aug_constants()

    seq_spec = lambda width: pl.BlockSpec((1, TOKEN_TILE, width), lambda b, s: (b, s, 0))
    y_pool, qt, qaugt, ka, vt = pl.pallas_call(
        _in_proj_kernel,
        grid=(B, S // TOKEN_TILE),
        in_specs=[seq_spec(D_MODEL), _resident((1, D_MODEL)),
                  _resident((D_MODEL, POOL_WIDTH)), _resident((D_MODEL, ATTN_WIDTH)),
                  _resident((2 * ATTN_WIDTH + F_ROWS, D_MODEL)), _resident((F_ROWS, 1)),
                  _resident((POOL_GROUPS, POOL_GROUP_DIM, POOL_GROUP_DIM)), _resident((1, POOL_WIDTH)),
                  _resident((1, POOL_WIDTH)), _resident((ATTN_WIDTH, 1)), _resident((1, ATTN_WIDTH)),
                  _resident((ATTN_WIDTH, ATTN_WIDTH)), _resident((TOKEN_TILE, TOKEN_TILE)),
                  _resident((LANES, N_HEAD_BLOCKS * LANES)), _resident((1, N_HEAD_BLOCKS * LANES))],
        out_specs=[seq_spec(POOL_WIDTH),
                   pl.BlockSpec((1, ATTN_WIDTH, TOKEN_TILE), lambda b, s: (b, 0, s)),
                   pl.BlockSpec((1, N_HEADS, BF16_ROWS, TOKEN_TILE), lambda b, s: (b, 0, 0, s)),
                   seq_spec(N_HEAD_BLOCKS * QK_DEPTH),
                   pl.BlockSpec((1, ATTN_WIDTH, TOKEN_TILE), lambda b, s: (b, 0, s))],
        out_shape=[jax.ShapeDtypeStruct((B, S, POOL_WIDTH), bf16),
                   jax.ShapeDtypeStruct((B, ATTN_WIDTH, S), bf16),
                   jax.ShapeDtypeStruct((B, N_HEADS, BF16_ROWS, S), bf16),
                   jax.ShapeDtypeStruct((B, S, N_HEAD_BLOCKS * QK_DEPTH), bf16),
                   jax.ShapeDtypeStruct((B, ATTN_WIDTH, S), bf16)],
        scratch_shapes=[pltpu.VMEM((POOL_HALO, POOL_WIDTH), f32), pltpu.VMEM((F_ROWS, LANES), f32)],
        compiler_params=params_2d,
        name="in_proj",
    )(x1.reshape(B, S, D_MODEL), row(mix_norm), w_pv, w_k, w_t, b_ft,
      pool_w.astype(bf16), row(pool_scale), row(out_norm_pool), q_gain, k_gain, headsum, triu, selk, ck)

    n_q = S // ATTN_TILE
    assert n_q % 2 == 0
    o_attn = pl.pallas_call(
        _attn_kernel,
        grid=(B, n_q // 2),
        in_specs=[pl.BlockSpec((1, ATTN_WIDTH, ATTN_TILE), lambda b, i: (b, 0, i)),
                  pl.BlockSpec((1, ATTN_WIDTH, ATTN_TILE), lambda b, i: (b, 0, n_q - 1 - i)),
                  pl.BlockSpec((1, N_HEADS, BF16_ROWS, ATTN_TILE), lambda b, i: (b, 0, 0, i)),
                  pl.BlockSpec((1, N_HEADS, BF16_ROWS, ATTN_TILE), lambda b, i: (b, 0, 0, n_q - 1 - i)),
                  pl.BlockSpec((1, S, N_HEAD_BLOCKS * QK_DEPTH), lambda b, i: (b, 0, 0)),
                  pl.BlockSpec((1, ATTN_WIDTH, S), lambda b, i: (b, 0, 0))],
        out_specs=pl.BlockSpec((1, 2, 1, ATTN_TILE, ATTN_WIDTH), lambda b, i: (b, 0, i, 0, 0)),
        out_shape=jax.ShapeDtypeStruct((B, 2, n_q // 2, ATTN_TILE, ATTN_WIDTH), f32),
        scratch_shapes=[pltpu.VMEM((N_HEADS, LANES, S), bf16),
                        pltpu.VMEM((2, N_HEADS, QK_DEPTH, ATTN_TILE), bf16),
                        pltpu.VMEM((2, 2, N_HEADS, ATTN_TILE, ATTN_TILE), f32),
                        pltpu.VMEM((2, N_HEADS, 1, ATTN_TILE), f32),
                        pltpu.VMEM((2, N_HEADS, LANES, ATTN_TILE), f32)],
        compiler_params=params_2d,
        name="attention",
    )(qt, qt, qaugt, qaugt, ka, vt)

    tiles_per_step = ffn_tile // ATTN_TILE
    steps_per_seq = S // ffn_tile

    def attn_tile_spec(which):
        def index(t):
            q = (t % steps_per_seq) * tiles_per_step + which
            high = q // (n_q // 2)
            return (t // steps_per_seq, high, jnp.where(high == 0, q, n_q - 1 - q), 0, 0)
        return pl.BlockSpec((1, 1, 1, ATTN_TILE, ATTN_WIDTH), index)

    half_spec = pl.BlockSpec((ffn_tile, POOL_WIDTH), lambda t: (t, 0))
    out = pl.pallas_call(
        _mix_ffn2_kernel,
        grid=(n_tok,),
        in_specs=[tok_spec, half_spec] + [attn_tile_spec(which) for which in range(tiles_per_step)]
                 + [_resident((1, ATTN_WIDTH)), _resident((D_MODEL, D_MODEL))] + _ffn_weight_specs(),
        out_specs=tok_spec,
        out_shape=jax.ShapeDtypeStruct((T, D_MODEL), f32),
        compiler_params=params_1d,
        name="mix_ffn2",
    )(x1, y_pool.reshape(T, POOL_WIDTH), *([o_attn] * tiles_per_step), row(out_norm_attn),
      w_out_bf, row(ffn2_norm), w_gate2, w_up2, w_down2)
    return out.reshape(B, S, D_MODEL)
```

```python
import math

import numpy as np
import jax
import jax.numpy as jnp
from jax import lax
from jax.experimental import pallas as pl
from jax.experimental.pallas import tpu as pltpu

D_MODEL = 1024
D_FF = 2816
POOL_WIDTH = 512
POOL_GROUPS = 4
POOL_GROUP_DIM = 128
POOL_WINDOWS = (2, 4, 8, 16)
ATTN_WIDTH = 512
HEAD_DIM = 64
N_HEADS = 8
EPS = 1e-6
LOG2E = math.log2(math.e)

LANES = 128
BF16_ROWS = 16
F_ROWS = BF16_ROWS
HEADS_PER_BLOCK = LANES // HEAD_DIM
N_HEAD_BLOCKS = N_HEADS // HEADS_PER_BLOCK
POOL_HALO = 16
N_SPLIT = 3
AUG_STRIDE = 2 * N_SPLIT
QK_DEPTH = 2 * LANES

TOKEN_TILE = 512
FFN_PASSES = 2
FF_CHUNK = 256
ATTN_TILE = 256
ATTN_TILE_GROUPS = ((7, 0), (6, 1), (5, 2), (4, 3))
VMEM_LIMIT = 56 * 1024 * 1024
MASK_VALUE = -1e30

bf16 = jnp.bfloat16
f32 = jnp.float32


def _rms_scale(x):
    return lax.rsqrt(jnp.mean(x * x, axis=-1, keepdims=True) + EPS)


def _swiglu_residual(x, norm_w, wg_ref, wu_ref, wd_ref):
    h = (x * norm_w).astype(bf16)
    r = _rms_scale(x)
    acc = jnp.zeros(x.shape, f32)
    for c in range(D_FF // FF_CHUNK):
        cols = slice(c * FF_CHUNK, (c + 1) * FF_CHUNK)
        g = r * jnp.dot(h, wg_ref[:, cols], preferred_element_type=f32)
        u = jnp.dot(h, wu_ref[:, cols], preferred_element_type=f32)
        a = (g * jax.nn.sigmoid(g) * u).astype(bf16)
        acc = acc + jnp.dot(a, wd_ref[cols, :], preferred_element_type=f32)
    return x + (0.5 * r) * acc


def _ffn1_kernel(x_ref, nw_ref, wg_ref, wu_ref, wd_ref, g2_ref, u2_ref, d2_ref, wo_ref,
                 o_ref, g2_bf_ref, u2_bf_ref, d2_bf_ref, wo_bf_ref):
    for src_ref, dst_ref in ((g2_ref, g2_bf_ref), (u2_ref, u2_bf_ref), (wo_ref, wo_bf_ref)):
        dst_ref[...] = src_ref[...].astype(bf16)

    @pl.when(pl.program_id(0) < D_FF // d2_ref.shape[0])
    def _():
        d2_bf_ref[...] = d2_ref[...].astype(bf16)

    for s in range(FFN_PASSES):
        rows = slice(s * TOKEN_TILE, (s + 1) * TOKEN_TILE)
        o_ref[rows, :] = _swiglu_residual(x_ref[rows, :], nw_ref[...], wg_ref, wu_ref, wd_ref)


def _mix_ffn2_kernel(x_ref, yp_ref, oa_ref, nattn_ref, wo_ref, nw_ref, wg_ref, wu_ref, wd_ref, o_ref):
    for s in range(FFN_PASSES):
        rows = slice(s * TOKEN_TILE, (s + 1) * TOKEN_TILE)
        oa = oa_ref[rows, :]
        ya = (oa * _rms_scale(oa) * nattn_ref[...]).astype(bf16)
        x2 = (x_ref[rows, :]
              + jnp.dot(yp_ref[rows, :], wo_ref[:POOL_WIDTH, :], preferred_element_type=f32)
              + jnp.dot(ya, wo_ref[POOL_WIDTH:, :], preferred_element_type=f32))
        o_ref[rows, :] = _swiglu_residual(x2, nw_ref[...], wg_ref, wu_ref, wd_ref)


def _split3(x):
    hi = x.astype(bf16).astype(f32)
    r = x - hi
    mid = r.astype(bf16).astype(f32)
    lo = (r - mid).astype(bf16).astype(f32)
    return hi, mid, lo


def _in_proj_kernel(x_ref, nw_ref, wpv_ref, wk_ref, wt_ref, bft_ref,
                    poolw_ref, pscale_ref, npool_ref, qn_ref, kn_ref, headsum_ref, triu_ref,
                    selk_ref, ck_ref,
                    yp_ref, qt_ref, qaugt_ref, ka_ref, vt_ref,
                    halo_ref, carry_ref):
    s_idx = pl.program_id(1)
    tm = x_ref.shape[1]

    @pl.when(s_idx == 0)
    def _():
        halo_ref[...] = jnp.zeros(halo_ref.shape, f32)
        carry_ref[...] = jnp.zeros(carry_ref.shape, f32)

    x = x_ref[0]
    h = (x * nw_ref[...]).astype(bf16)
    r = _rms_scale(x)
    r_row = jnp.broadcast_to(r, (tm, LANES)).T[0:1]

    qvf = r_row * lax.dot_general(wt_ref[...], h, (((1,), (1,)), ((), ())), preferred_element_type=f32)
    pv = r * jnp.dot(h, wpv_ref[...], preferred_element_type=f32)
    t = r * jnp.dot(h, wk_ref[...], preferred_element_type=f32)

    fl = qvf[2 * ATTN_WIDTH:] + bft_ref[...]
    logf = (jnp.minimum(fl, 0.0) - jnp.log1p(jnp.exp(-jnp.abs(fl)))) * LOG2E
    logf = jnp.where(lax.broadcasted_iota(jnp.int32, logf.shape, 0) < N_HEADS, logf, 0.0)
    parts = jnp.concatenate(_split3(logf), axis=0).astype(bf16)
    sums = jnp.dot(parts, triu_ref[...], preferred_element_type=f32)
    fcum_t = (sums[:F_ROWS] + sums[F_ROWS:2 * F_ROWS] + sums[2 * F_ROWS:]) + carry_ref[:, 0:1]
    carry_ref[...] = jnp.broadcast_to(fcum_t[:, tm - 1:tm], carry_ref.shape)
    hi_t, mid_t, lo_t = (part[:N_HEADS] for part in _split3(fcum_t))

    full = jnp.concatenate([halo_ref[...], pv], axis=0)
    halo_ref[...] = pv[tm - POOL_HALO:, :]
    pos = s_idx * tm + lax.broadcasted_iota(jnp.int32, (tm, 1), 0)
    mixed = []
    for g in range(POOL_GROUPS):
        w = POOL_WINDOWS[g]
        cols = slice(g * POOL_GROUP_DIM, (g + 1) * POOL_GROUP_DIM)
        win = full[:, cols]
        shift = 1
        while shift < w:
            win = win + pltpu.roll(win, shift, 0)
            shift *= 2
        count = jnp.minimum(pos + 1, w).astype(f32)
        pooled = win[POOL_HALO:, :] / count - pv[:, cols]
        mixed.append(jnp.dot(pooled.astype(bf16), poolw_ref[g], preferred_element_type=f32))
    mixed = jnp.concatenate(mixed, axis=1) * pscale_ref[...]
    yp_ref[0] = (mixed * _rms_scale(mixed) * npool_ref[...]).astype(bf16)

    ssq = jnp.dot((t * t).astype(bf16), headsum_ref[...], preferred_element_type=f32)
    kn = (t * lax.rsqrt(ssq * (1.0 / HEAD_DIM) + EPS) * kn_ref[...]).astype(bf16)

    packed_t = jnp.concatenate([hi_t, mid_t, lo_t, jnp.zeros((LANES - N_SPLIT * N_HEADS, tm), f32)], axis=0)
    packed = packed_t.T.astype(bf16)
    kaug = (jnp.dot(packed, selk_ref[...], preferred_element_type=f32) + ck_ref[...]).astype(bf16)
    pieces = []
    for p in range(N_HEAD_BLOCKS):
        cols = slice(p * LANES, (p + 1) * LANES)
        pieces += [kn[:, cols], kaug[:, cols]]
    ka_ref[0] = jnp.concatenate(pieces, axis=1)

    qt = qvf[:ATTN_WIDTH].reshape(N_HEADS, HEAD_DIM, tm)
    qt = qt * lax.rsqrt(jnp.mean(qt * qt, axis=1, keepdims=True) + EPS)
    qt_ref[0] = (qt.reshape(ATTN_WIDTH, tm) * qn_ref[...]).astype(bf16)
    vt_ref[0] = qvf[ATTN_WIDTH:2 * ATTN_WIDTH].astype(bf16)

    rowid = lax.broadcasted_iota(jnp.int32, (BF16_ROWS, tm), 0)
    for hh in range(N_HEADS):
        base = (hh % HEADS_PER_BLOCK) * AUG_STRIDE
        ones_rows = ((rowid >= base + N_SPLIT) & (rowid < base + 2 * N_SPLIT)).astype(f32)
        aug = jnp.where(rowid == base, hi_t[hh:hh + 1],
                        jnp.where(rowid == base + 1, mid_t[hh:hh + 1],
                                  jnp.where(rowid == base + 2, lo_t[hh:hh + 1], ones_rows)))
        qaugt_ref[0, hh] = aug.astype(bf16)


def _attn_kernel(qt_ref, qaugt_ref, ka_ref, vt_ref, o_ref, vth_ref, qat_ref, s_ref, m_ref, acc_ref):
    tq = qat_ref.shape[3]
    tk = tq
    n_tiles = ka_ref.shape[1] // tk

    def build_values_operand(hh):
        seq = vt_ref.shape[2]
        ones_then_zeros = (lax.broadcasted_iota(jnp.int32, (HEAD_DIM, seq), 0) == 0).astype(bf16)
        vh = vt_ref[0, hh * HEAD_DIM:(hh + 1) * HEAD_DIM, :]
        odd = hh % HEADS_PER_BLOCK
        vth_ref[hh, 0:HEAD_DIM, :] = ones_then_zeros if odd else vh
        vth_ref[hh, HEAD_DIM:LANES, :] = vh if odd else ones_then_zeros

    def build_query_operand(w, i, hh):
        queries = slice(i * tq, (i + 1) * tq)
        zeros_head = jnp.zeros((HEAD_DIM, tq), bf16)
        qh = qt_ref[0, hh * HEAD_DIM:(hh + 1) * HEAD_DIM, queries]
        odd = hh % HEADS_PER_BLOCK
        qat_ref[w, hh, 0:HEAD_DIM, :] = zeros_head if odd else qh
        qat_ref[w, hh, HEAD_DIM:LANES, :] = qh if odd else zeros_head
        qat_ref[w, hh, LANES:LANES + BF16_ROWS, :] = qaugt_ref[0, hh, :, queries]
        qat_ref[w, hh, LANES + BF16_ROWS:, :] = jnp.zeros((QK_DEPTH - LANES - BF16_ROWS, tq), bf16)

    def keys(j):
        return slice(j * tk, (j + 1) * tk)

    def logits(w, i, j, hh):
        p = hh // HEADS_PER_BLOCK
        ka = ka_ref[0, keys(j), p * QK_DEPTH:(p + 1) * QK_DEPTH]
        st = jnp.dot(ka, qat_ref[w, hh], preferred_element_type=f32)
        if j == i:
            key_minus_query = (lax.broadcasted_iota(jnp.int32, (tk, tq), 0)
                               - lax.broadcasted_iota(jnp.int32, (tk, tq), 1))
            st = jnp.where(key_minus_query > 0, MASK_VALUE, st)
        s_ref[w, j % 2, hh] = st

    def softmax_values(w, j, hh):
        st_ref = s_ref.at[w, j % 2, hh]
        vt = vth_ref[hh, :, keys(j)]
        m_blk = jnp.max(st_ref[...], axis=0, keepdims=True)
        if j == 0:
            m_ref[w, hh] = m_blk
            pt = jnp.exp2(st_ref[...] - m_blk).astype(bf16)
            acc_ref[w, hh] = jnp.dot(vt, pt, preferred_element_type=f32)
        else:
            m_prev = m_ref[w, hh]
            m_new = jnp.maximum(m_prev, m_blk)
            m_ref[w, hh] = m_new
            pt = jnp.exp2(st_ref[...] - m_new).astype(bf16)
            acc_ref[w, hh] = (jnp.exp2(m_prev - m_new) * acc_ref[w, hh]
                              + jnp.dot(vt, pt, preferred_element_type=f32))

    def write_output(w, i, p):
        row = lax.broadcasted_iota(jnp.int32, (LANES, tq), 0)
        even, odd = acc_ref[w, 2 * p], acc_ref[w, 2 * p + 1]
        ot = jnp.where(row < HEAD_DIM, even / even[HEAD_DIM:HEAD_DIM + 1, :], odd / odd[0:1, :])
        o_ref[0, i * tq:(i + 1) * tq, p * LANES:(p + 1) * LANES] = ot.T

    def run(tiles, first_group):
        def start(pos):
            for hh in range(N_HEADS):
                build_query_operand(pos % 2, tiles[pos], hh)
                logits(pos % 2, tiles[pos], 0, hh)

        def key_block(pos, j):
            w, i = pos % 2, tiles[pos]
            for hh in range(N_HEADS):
                if first_group and pos == 0 and j == 0:
                    build_values_operand(hh)
                softmax_values(w, j, hh)
                if j < i:
                    logits(w, i, j + 1, hh)
                elif hh % HEADS_PER_BLOCK == HEADS_PER_BLOCK - 1:
                    write_output(w, i, hh // HEADS_PER_BLOCK)

        start(0)
        for pos, i in enumerate(tiles):
            for j in range(i + 1):
                if j == i and pos + 1 < len(tiles):
                    start(pos + 1)
                key_block(pos, j)

    assert sorted(i for tiles in ATTN_TILE_GROUPS for i in tiles) == list(range(n_tiles))
    for gi, tiles in enumerate(ATTN_TILE_GROUPS):
        @pl.when(pl.program_id(1) == gi)
        def _(tiles=tiles, first_group=(gi == 0)):
            run(tiles, first_group)


def _resident(shape):
    nd = len(shape)
    return pl.BlockSpec(shape, lambda *_: (0,) * nd, pipeline_mode=pl.Buffered(1))


def _ffn_weight_specs():
    return [_resident((1, D_MODEL)), _resident((D_MODEL, D_FF)), _resident((D_MODEL, D_FF)),
            _resident((D_FF, D_MODEL))]


def _k_aug_constants():
    selk = np.zeros((LANES, N_HEAD_BLOCKS * LANES), np.float32)
    ck = np.zeros((1, N_HEAD_BLOCKS * LANES), np.float32)
    for hh in range(N_HEADS):
        p, odd = divmod(hh, HEADS_PER_BLOCK)
        base = odd * AUG_STRIDE
        for part in range(N_SPLIT):
            ck[0, p * LANES + base + part] = 1.0
            selk[part * N_HEADS + hh, p * LANES + base + N_SPLIT + part] = -1.0
    return jnp.asarray(selk, bf16), jnp.asarray(ck)


def kernel(x, ffn1_norm, ffn1_w_gate, ffn1_w_up, ffn1_w_down, mix_norm, w_in, b_forget, pool_w, pool_scale, q_norm, k_norm, out_norm_pool, out_norm_attn, w_out, ffn2_norm, ffn2_w_gate, ffn2_w_up, ffn2_w_down):
    B, S, D = x.shape
    ffn_tile = FFN_PASSES * TOKEN_TILE
    assert D == D_MODEL and S % ffn_tile == 0 and S % ATTN_TILE == 0
    T = B * S
    n_tok = T // ffn_tile
    row = lambda a: a.reshape(1, -1).astype(f32)

    tok_spec = pl.BlockSpec((ffn_tile, D_MODEL), lambda t: (t, 0))
    params_1d = pltpu.CompilerParams(dimension_semantics=("arbitrary",), vmem_limit_bytes=VMEM_LIMIT)
    params_2d = pltpu.CompilerParams(dimension_semantics=("arbitrary", "arbitrary"), vmem_limit_bytes=VMEM_LIMIT)

    assert D_MODEL % n_tok == 0 and (D_MODEL // n_tok) % BF16_ROWS == 0
    slab = D_MODEL // n_tok
    down_slab = BF16_ROWS * (D_FF // FF_CHUNK)
    n_down = D_FF // down_slab
    assert n_down * down_slab == D_FF and n_down <= n_tok
    ff_slab = pl.BlockSpec((slab, D_FF), lambda t: (t, 0))
    wd_slab = pl.BlockSpec((down_slab, D_MODEL), lambda t: (jnp.minimum(t, n_down - 1), 0))
    wo_slab = pl.BlockSpec((slab, D_MODEL), lambda t: (t, 0))
    x1, w_gate2, w_up2, w_down2, w_out_bf = pl.pallas_call(
        _ffn1_kernel,
        grid=(n_tok,),
        in_specs=[tok_spec] + _ffn_weight_specs() + [ff_slab, ff_slab, wd_slab, wo_slab],
        out_specs=[tok_spec, ff_slab, ff_slab, wd_slab, wo_slab],
        out_shape=[jax.ShapeDtypeStruct((T, D_MODEL), f32),
                   jax.ShapeDtypeStruct((D_MODEL, D_FF), bf16), jax.ShapeDtypeStruct((D_MODEL, D_FF), bf16),
                   jax.ShapeDtypeStruct((D_FF, D_MODEL), bf16), jax.ShapeDtypeStruct((D_MODEL, D_MODEL), bf16)],
        compiler_params=params_1d,
        name="ffn1",
    )(x.reshape(T, D_MODEL), row(ffn1_norm), ffn1_w_gate.astype(bf16), ffn1_w_up.astype(bf16),
      ffn1_w_down.astype(bf16), ffn2_w_gate, ffn2_w_up, ffn2_w_down, w_out)

    c0 = POOL_WIDTH
    w_pv = w_in[:, :c0].astype(bf16)
    w_k = w_in[:, c0 + ATTN_WIDTH:c0 + 2 * ATTN_WIDTH].astype(bf16)
    w_t = jnp.concatenate([w_in[:, c0:c0 + ATTN_WIDTH], w_in[:, c0 + 2 * ATTN_WIDTH:],
                           jnp.zeros((D_MODEL, F_ROWS - N_HEADS), w_in.dtype)], axis=1).T.astype(bf16)
    b_ft = jnp.pad(b_forget.astype(f32), (0, F_ROWS - N_HEADS)).reshape(F_ROWS, 1)
    q_gain = (jnp.tile(q_norm, N_HEADS).astype(f32) * (LOG2E / math.sqrt(HEAD_DIM))).reshape(ATTN_WIDTH, 1)
    k_gain = row(jnp.tile(k_norm, N_HEADS))
    head_id = jnp.arange(ATTN_WIDTH) // HEAD_DIM
    headsum = (head_id[:, None] == head_id[None, :]).astype(bf16)
    triu = (jnp.arange(TOKEN_TILE)[:, None] <= jnp.arange(TOKEN_TILE)[None, :]).astype(bf16)
    selk, ck = _k_aug_constants()

    seq_spec = lambda width: pl.BlockSpec((1, TOKEN_TILE, width), lambda b, s: (b, s, 0))
    y_pool, qt, qaugt, ka, vt = pl.pallas_call(
        _in_proj_kernel,
        grid=(B, S // TOKEN_TILE),
        in_specs=[seq_spec(D_MODEL), _resident((1, D_MODEL)),
                  _resident((D_MODEL, POOL_WIDTH)), _resident((D_MODEL, ATTN_WIDTH)),
                  _resident((2 * ATTN_WIDTH + F_ROWS, D_MODEL)), _resident((F_ROWS, 1)),
                  _resident((POOL_GROUPS, POOL_GROUP_DIM, POOL_GROUP_DIM)), _resident((1, POOL_WIDTH)),
                  _resident((1, POOL_WIDTH)), _resident((ATTN_WIDTH, 1)), _resident((1, ATTN_WIDTH)),
                  _resident((ATTN_WIDTH, ATTN_WIDTH)), _resident((TOKEN_TILE, TOKEN_TILE)),
                  _resident((LANES, N_HEAD_BLOCKS * LANES)), _resident((1, N_HEAD_BLOCKS * LANES))],
        out_specs=[seq_spec(POOL_WIDTH),
                   pl.BlockSpec((1, ATTN_WIDTH, TOKEN_TILE), lambda b, s: (b, 0, s)),
                   pl.BlockSpec((1, N_HEADS, BF16_ROWS, TOKEN_TILE), lambda b, s: (b, 0, 0, s)),
                   seq_spec(N_HEAD_BLOCKS * QK_DEPTH),
                   pl.BlockSpec((1, ATTN_WIDTH, TOKEN_TILE), lambda b, s: (b, 0, s))],
        out_shape=[jax.ShapeDtypeStruct((B, S, POOL_WIDTH), bf16),
                   jax.ShapeDtypeStruct((B, ATTN_WIDTH, S), bf16),
                   jax.ShapeDtypeStruct((B, N_HEADS, BF16_ROWS, S), bf16),
                   jax.ShapeDtypeStruct((B, S, N_HEAD_BLOCKS * QK_DEPTH), bf16),
                   jax.ShapeDtypeStruct((B, ATTN_WIDTH, S), bf16)],
        scratch_shapes=[pltpu.VMEM((POOL_HALO, POOL_WIDTH), f32), pltpu.VMEM((F_ROWS, LANES), f32)],
        compiler_params=params_2d,
        name="in_proj",
    )(x1.reshape(B, S, D_MODEL), row(mix_norm), w_pv, w_k, w_t, b_ft,
      pool_w.astype(bf16), row(pool_scale), row(out_norm_pool), q_gain, k_gain, headsum, triu, selk, ck)

    assert S == ATTN_TILE * sum(len(tiles) for tiles in ATTN_TILE_GROUPS)
    seq_rows = lambda width: pl.BlockSpec((1, S, width), lambda b, g: (b, 0, 0))
    seq_lanes = lambda rows: pl.BlockSpec((1, rows, S), lambda b, g: (b, 0, 0))
    o_attn = pl.pallas_call(
        _attn_kernel,
        grid=(B, len(ATTN_TILE_GROUPS)),
        in_specs=[seq_lanes(ATTN_WIDTH), pl.BlockSpec((1, N_HEADS, BF16_ROWS, S), lambda b, g: (b, 0, 0, 0)),
                  seq_rows(N_HEAD_BLOCKS * QK_DEPTH), seq_lanes(ATTN_WIDTH)],
        out_specs=seq_rows(ATTN_WIDTH),
        out_shape=jax.ShapeDtypeStruct((B, S, ATTN_WIDTH), f32),
        scratch_shapes=[pltpu.VMEM((N_HEADS, LANES, S), bf16),
                        pltpu.VMEM((2, N_HEADS, QK_DEPTH, ATTN_TILE), bf16),
                        pltpu.VMEM((2, 2, N_HEADS, ATTN_TILE, ATTN_TILE), f32),
                        pltpu.VMEM((2, N_HEADS, 1, ATTN_TILE), f32),
                        pltpu.VMEM((2, N_HEADS, LANES, ATTN_TILE), f32)],
        compiler_params=params_2d,
        name="attention",
    )(qt, qaugt, ka, vt)

    half_spec = pl.BlockSpec((ffn_tile, POOL_WIDTH), lambda t: (t, 0))
    out = pl.pallas_call(
        _mix_ffn2_kernel,
        grid=(n_tok,),
        in_specs=[tok_spec, half_spec, half_spec, _resident((1, ATTN_WIDTH)),
                  _resident((D_MODEL, D_MODEL))] + _ffn_weight_specs(),
        out_specs=tok_spec,
        out_shape=jax.ShapeDtypeStruct((T, D_MODEL), f32),
        compiler_params=params_1d,
        name="mix_ffn2",
    )(x1, y_pool.reshape(T, POOL_WIDTH), o_attn.reshape(T, ATTN_WIDTH), row(out_norm_attn),
      w_out_bf, row(ffn2_norm), w_gate2, w_up2, w_down2)
    return out.reshape(B, S, D_MODEL)
```

```python
import math

import numpy as np
import jax
import jax.numpy as jnp
from jax import lax
from jax.experimental import pallas as pl
from jax.experimental.pallas import tpu as pltpu

D_MODEL = 1024
D_FF = 2816
POOL_WIDTH = 512
POOL_GROUPS = 4
POOL_GROUP_DIM = 128
POOL_WINDOWS = (2, 4, 8, 16)
ATTN_WIDTH = 512
HEAD_DIM = 64
N_HEADS = 8
EPS = 1e-6
LOG2E = math.log2(math.e)

LANES = 128
BF16_ROWS = 16
F_ROWS = BF16_ROWS
HEADS_PER_BLOCK = LANES // HEAD_DIM
N_HEAD_BLOCKS = N_HEADS // HEADS_PER_BLOCK
POOL_HALO = 16
N_SPLIT = 3
AUG_STRIDE = 2 * N_SPLIT
QK_DEPTH = 2 * LANES

TOKEN_TILE = 512
FFN_PASSES = 2
FF_CHUNK = 256
ATTN_TILE = 256
ATTN_TILE_GROUPS = ((7, 0, 6, 1), (5, 2, 4, 3))
VMEM_LIMIT = 56 * 1024 * 1024
MASK_VALUE = -1e30

bf16 = jnp.bfloat16
f32 = jnp.float32


def _rms_scale(x):
    return lax.rsqrt(jnp.mean(x * x, axis=-1, keepdims=True) + EPS)


def _swiglu_residual(x, norm_w, wg_ref, wu_ref, wd_ref):
    h = (x * norm_w).astype(bf16)
    r = _rms_scale(x)
    acc = jnp.zeros(x.shape, f32)
    for c in range(D_FF // FF_CHUNK):
        cols = slice(c * FF_CHUNK, (c + 1) * FF_CHUNK)
        g = r * jnp.dot(h, wg_ref[:, cols], preferred_element_type=f32)
        u = jnp.dot(h, wu_ref[:, cols], preferred_element_type=f32)
        a = (g * jax.nn.sigmoid(g) * u).astype(bf16)
        acc = acc + jnp.dot(a, wd_ref[cols, :], preferred_element_type=f32)
    return x + (0.5 * r) * acc


def _ffn1_kernel(x_ref, nw_ref, wg_ref, wu_ref, wd_ref, g2_ref, u2_ref, d2_ref, wo_ref,
                 o_ref, g2_bf_ref, u2_bf_ref, d2_bf_ref, wo_bf_ref):
    for src_ref, dst_ref in ((g2_ref, g2_bf_ref), (u2_ref, u2_bf_ref), (wo_ref, wo_bf_ref)):
        dst_ref[...] = src_ref[...].astype(bf16)

    @pl.when(pl.program_id(0) < D_FF // d2_ref.shape[0])
    def _():
        d2_bf_ref[...] = d2_ref[...].astype(bf16)

    for s in range(FFN_PASSES):
        rows = slice(s * TOKEN_TILE, (s + 1) * TOKEN_TILE)
        o_ref[rows, :] = _swiglu_residual(x_ref[rows, :], nw_ref[...], wg_ref, wu_ref, wd_ref)


def _mix_ffn2_kernel(x_ref, yp_ref, oa_ref, nattn_ref, wo_ref, nw_ref, wg_ref, wu_ref, wd_ref, o_ref):
    for s in range(FFN_PASSES):
        rows = slice(s * TOKEN_TILE, (s + 1) * TOKEN_TILE)
        oa = oa_ref[rows, :]
        ya = (oa * _rms_scale(oa) * nattn_ref[...]).astype(bf16)
        x2 = (x_ref[rows, :]
              + jnp.dot(yp_ref[rows, :], wo_ref[:POOL_WIDTH, :], preferred_element_type=f32)
              + jnp.dot(ya, wo_ref[POOL_WIDTH:, :], preferred_element_type=f32))
        o_ref[rows, :] = _swiglu_residual(x2, nw_ref[...], wg_ref, wu_ref, wd_ref)


def _split3(x):
    hi = x.astype(bf16).astype(f32)
    r = x - hi
    mid = r.astype(bf16).astype(f32)
    lo = (r - mid).astype(bf16).astype(f32)
    return hi, mid, lo


def _in_proj_kernel(x_ref, nw_ref, wpv_ref, wk_ref, wt_ref, bft_ref,
                    poolw_ref, pscale_ref, npool_ref, qn_ref, kn_ref, headsum_ref, triu_ref,
                    selk_ref, ck_ref,
                    yp_ref, qt_ref, qaugt_ref, ka_ref, vt_ref,
                    halo_ref, carry_ref):
    s_idx = pl.program_id(1)
    tm = x_ref.shape[1]

    @pl.when(s_idx == 0)
    def _():
        halo_ref[...] = jnp.zeros(halo_ref.shape, f32)
        carry_ref[...] = jnp.zeros(carry_ref.shape, f32)

    x = x_ref[0]
    h = (x * nw_ref[...]).astype(bf16)
    r = _rms_scale(x)
    r_row = jnp.broadcast_to(r, (tm, LANES)).T[0:1]

    qvf = r_row * lax.dot_general(wt_ref[...], h, (((1,), (1,)), ((), ())), preferred_element_type=f32)
    pv = r * jnp.dot(h, wpv_ref[...], preferred_element_type=f32)
    t = r * jnp.dot(h, wk_ref[...], preferred_element_type=f32)

    fl = qvf[2 * ATTN_WIDTH:] + bft_ref[...]
    logf = (jnp.minimum(fl, 0.0) - jnp.log1p(jnp.exp(-jnp.abs(fl)))) * LOG2E
    logf = jnp.where(lax.broadcasted_iota(jnp.int32, logf.shape, 0) < N_HEADS, logf, 0.0)
    parts = jnp.concatenate(_split3(logf), axis=0).astype(bf16)
    sums = jnp.dot(parts, triu_ref[...], preferred_element_type=f32)
    fcum_t = (sums[:F_ROWS] + sums[F_ROWS:2 * F_ROWS] + sums[2 * F_ROWS:]) + carry_ref[:, 0:1]
    carry_ref[...] = jnp.broadcast_to(fcum_t[:, tm - 1:tm], carry_ref.shape)
    hi_t, mid_t, lo_t = (part[:N_HEADS] for part in _split3(fcum_t))

    full = jnp.concatenate([halo_ref[...], pv], axis=0)
    halo_ref[...] = pv[tm - POOL_HALO:, :]
    pos = s_idx * tm + lax.broadcasted_iota(jnp.int32, (tm, 1), 0)
    mixed = []
    for g in range(POOL_GROUPS):
        w = POOL_WINDOWS[g]
        cols = slice(g * POOL_GROUP_DIM, (g + 1) * POOL_GROUP_DIM)
        win = full[:, cols]
        shift = 1
        while shift < w:
            win = win + pltpu.roll(win, shift, 0)
            shift *= 2
        count = jnp.minimum(pos + 1, w).astype(f32)
        pooled = win[POOL_HALO:, :] / count - pv[:, cols]
        mixed.append(jnp.dot(pooled.astype(bf16), poolw_ref[g], preferred_element_type=f32))
    mixed = jnp.concatenate(mixed, axis=1) * pscale_ref[...]
    yp_ref[0] = (mixed * _rms_scale(mixed) * npool_ref[...]).astype(bf16)

    ssq = jnp.dot((t * t).astype(bf16), headsum_ref[...], preferred_element_type=f32)
    kn = (t * lax.rsqrt(ssq * (1.0 / HEAD_DIM) + EPS) * kn_ref[...]).astype(bf16)

    packed_t = jnp.concatenate([hi_t, mid_t, lo_t, jnp.zeros((LANES - N_SPLIT * N_HEADS, tm), f32)], axis=0)
    packed = packed_t.T.astype(bf16)
    kaug = (jnp.dot(packed, selk_ref[...], preferred_element_type=f32) + ck_ref[...]).astype(bf16)
    pieces = []
    for p in range(N_HEAD_BLOCKS):
        cols = slice(p * LANES, (p + 1) * LANES)
        pieces += [kn[:, cols], kaug[:, cols]]
    ka_ref[0] = jnp.concatenate(pieces, axis=1)

    qt = qvf[:ATTN_WIDTH].reshape(N_HEADS, HEAD_DIM, tm)
    qt = qt * lax.rsqrt(jnp.mean(qt * qt, axis=1, keepdims=True) + EPS)
    qt_ref[0] = (qt.reshape(ATTN_WIDTH, tm) * qn_ref[...]).astype(bf16)
    vt_ref[0] = qvf[ATTN_WIDTH:2 * ATTN_WIDTH].astype(bf16)

    rowid = lax.broadcasted_iota(jnp.int32, (BF16_ROWS, tm), 0)
    for hh in range(N_HEADS):
        base = (hh % HEADS_PER_BLOCK) * AUG_STRIDE
        ones_rows = ((rowid >= base + N_SPLIT) & (rowid < base + 2 * N_SPLIT)).astype(f32)
        aug = jnp.where(rowid == base, hi_t[hh:hh + 1],
                        jnp.where(rowid == base + 1, mid_t[hh:hh + 1],
                                  jnp.where(rowid == base + 2, lo_t[hh:hh + 1], ones_rows)))
        qaugt_ref[0, hh] = aug.astype(bf16)


def _attn_kernel(qt_ref, qaugt_ref, ka_ref, vt_ref, o_ref, vth_ref, qat_ref, s_ref, m_ref, acc_ref):
    tq = qat_ref.shape[3]
    tk = tq
    n_tiles = ka_ref.shape[1] // tk

    def build_values_operand(hh):
        seq = vt_ref.shape[2]
        ones_then_zeros = (lax.broadcasted_iota(jnp.int32, (HEAD_DIM, seq), 0) == 0).astype(bf16)
        vh = vt_ref[0, hh * HEAD_DIM:(hh + 1) * HEAD_DIM, :]
        odd = hh % HEADS_PER_BLOCK
        vth_ref[hh, 0:HEAD_DIM, :] = ones_then_zeros if odd else vh
        vth_ref[hh, HEAD_DIM:LANES, :] = vh if odd else ones_then_zeros

    def build_query_operand(pos, i, hh):
        queries = slice(i * tq, (i + 1) * tq)
        zeros_head = jnp.zeros((HEAD_DIM, tq), bf16)
        qh = qt_ref[0, hh * HEAD_DIM:(hh + 1) * HEAD_DIM, queries]
        odd = hh % HEADS_PER_BLOCK
        qat_ref[pos, hh, 0:HEAD_DIM, :] = zeros_head if odd else qh
        qat_ref[pos, hh, HEAD_DIM:LANES, :] = qh if odd else zeros_head
        qat_ref[pos, hh, LANES:LANES + BF16_ROWS, :] = qaugt_ref[0, hh, :, queries]
        qat_ref[pos, hh, LANES + BF16_ROWS:, :] = jnp.zeros((QK_DEPTH - LANES - BF16_ROWS, tq), bf16)

    def keys(j):
        return slice(j * tk, (j + 1) * tk)

    def logits(pos, i, j, hh):
        w = pos % 2
        p = hh // HEADS_PER_BLOCK
        ka = ka_ref[0, keys(j), p * QK_DEPTH:(p + 1) * QK_DEPTH]
        st = jnp.dot(ka, qat_ref[pos, hh], preferred_element_type=f32)
        if j == i:
            key_minus_query = (lax.broadcasted_iota(jnp.int32, (tk, tq), 0)
                               - lax.broadcasted_iota(jnp.int32, (tk, tq), 1))
            st = jnp.where(key_minus_query > 0, MASK_VALUE, st)
        s_ref[w, j % 2, hh] = st

    def softmax_values(w, j, hh):
        st_ref = s_ref.at[w, j % 2, hh]
        vt = vth_ref[hh, :, keys(j)]
        m_blk = jnp.max(st_ref[...], axis=0, keepdims=True)
        if j == 0:
            m_ref[w, hh] = m_blk
            pt = jnp.exp2(st_ref[...] - m_blk).astype(bf16)
            acc_ref[w, hh] = jnp.dot(vt, pt, preferred_element_type=f32)
        else:
            m_prev = m_ref[w, hh]
            m_new = jnp.maximum(m_prev, m_blk)
            m_ref[w, hh] = m_new
            pt = jnp.exp2(st_ref[...] - m_new).astype(bf16)
            acc_ref[w, hh] = (jnp.exp2(m_prev - m_new) * acc_ref[w, hh]
                              + jnp.dot(vt, pt, preferred_element_type=f32))

    def write_output(w, i, p):
        row = lax.broadcasted_iota(jnp.int32, (LANES, tq), 0)
        even, odd = acc_ref[w, 2 * p], acc_ref[w, 2 * p + 1]
        ot = jnp.where(row < HEAD_DIM, even / even[HEAD_DIM:HEAD_DIM + 1, :], odd / odd[0:1, :])
        o_ref[0, i * tq:(i + 1) * tq, p * LANES:(p + 1) * LANES] = ot.T

    def run(tiles, first_group):
        def start(pos):
            for hh in range(N_HEADS):
                build_query_operand(pos, tiles[pos], hh)
                logits(pos, tiles[pos], 0, hh)

        def key_block(pos, j):
            w, i = pos % 2, tiles[pos]
            for hh in range(N_HEADS):
                if first_group and pos == 0 and j == 0:
                    build_values_operand(hh)
                softmax_values(w, j, hh)
                if j < i:
                    logits(pos, i, j + 1, hh)
                elif hh % HEADS_PER_BLOCK == HEADS_PER_BLOCK - 1:
                    write_output(w, i, hh // HEADS_PER_BLOCK)

        start(0)
        for pos, i in enumerate(tiles):
            for j in range(i + 1):
                if j == i and pos + 1 < len(tiles):
                    start(pos + 1)
                key_block(pos, j)

    assert sorted(i for tiles in ATTN_TILE_GROUPS for i in tiles) == list(range(n_tiles))
    for gi, tiles in enumerate(ATTN_TILE_GROUPS):
        @pl.when(pl.program_id(1) == gi)
        def _(tiles=tiles, first_group=(gi == 0)):
            run(tiles, first_group)


def _resident(shape):
    nd = len(shape)
    return pl.BlockSpec(shape, lambda *_: (0,) * nd, pipeline_mode=pl.Buffered(1))


def _ffn_weight_specs():
    return [_resident((1, D_MODEL)), _resident((D_MODEL, D_FF)), _resident((D_MODEL, D_FF)),
            _resident((D_FF, D_MODEL))]


def _k_aug_constants():
    selk = np.zeros((LANES, N_HEAD_BLOCKS * LANES), np.float32)
    ck = np.zeros((1, N_HEAD_BLOCKS * LANES), np.float32)
    for hh in range(N_HEADS):
        p, odd = divmod(hh, HEADS_PER_BLOCK)
        base = odd * AUG_STRIDE
        for part in range(N_SPLIT):
            ck[0, p * LANES + base + part] = 1.0
            selk[part * N_HEADS + hh, p * LANES + base + N_SPLIT + part] = -1.0
    return jnp.asarray(selk, bf16), jnp.asarray(ck)


def kernel(x, ffn1_norm, ffn1_w_gate, ffn1_w_up, ffn1_w_down, mix_norm, w_in, b_forget, pool_w, pool_scale, q_norm, k_norm, out_norm_pool, out_norm_attn, w_out, ffn2_norm, ffn2_w_gate, ffn2_w_up, ffn2_w_down):
    B, S, D = x.shape
    ffn_tile = FFN_PASSES * TOKEN_TILE
    assert D == D_MODEL and S % ffn_tile == 0 and S % ATTN_TILE == 0
    T = B * S
    n_tok = T // ffn_tile
    row = lambda a: a.reshape(1, -1).astype(f32)

    tok_spec = pl.BlockSpec((ffn_tile, D_MODEL), lambda t: (t, 0))
    params_1d = pltpu.CompilerParams(dimension_semantics=("arbitrary",), vmem_limit_bytes=VMEM_LIMIT)
    params_2d = pltpu.CompilerParams(dimension_semantics=("arbitrary", "arbitrary"), vmem_limit_bytes=VMEM_LIMIT)

    assert D_MODEL % n_tok == 0 and (D_MODEL // n_tok) % BF16_ROWS == 0
    slab = D_MODEL // n_tok
    down_slab = BF16_ROWS * (D_FF // FF_CHUNK)
    n_down = D_FF // down_slab
    assert n_down * down_slab == D_FF and n_down <= n_tok
    ff_slab = pl.BlockSpec((slab, D_FF), lambda t: (t, 0))
    wd_slab = pl.BlockSpec((down_slab, D_MODEL), lambda t: (jnp.minimum(t, n_down - 1), 0))
    wo_slab = pl.BlockSpec((slab, D_MODEL), lambda t: (t, 0))
    x1, w_gate2, w_up2, w_down2, w_out_bf = pl.pallas_call(
        _ffn1_kernel,
        grid=(n_tok,),
        in_specs=[tok_spec] + _ffn_weight_specs() + [ff_slab, ff_slab, wd_slab, wo_slab],
        out_specs=[tok_spec, ff_slab, ff_slab, wd_slab, wo_slab],
        out_shape=[jax.ShapeDtypeStruct((T, D_MODEL), f32),
                   jax.ShapeDtypeStruct((D_MODEL, D_FF), bf16), jax.ShapeDtypeStruct((D_MODEL, D_FF), bf16),
                   jax.ShapeDtypeStruct((D_FF, D_MODEL), bf16), jax.ShapeDtypeStruct((D_MODEL, D_MODEL), bf16)],
        compiler_params=params_1d,
        name="ffn1",
    )(x.reshape(T, D_MODEL), row(ffn1_norm), ffn1_w_gate.astype(bf16), ffn1_w_up.astype(bf16),
      ffn1_w_down.astype(bf16), ffn2_w_gate, ffn2_w_up, ffn2_w_down, w_out)

    c0 = POOL_WIDTH
    w_pv = w_in[:, :c0].astype(bf16)
    w_k = w_in[:, c0 + ATTN_WIDTH:c0 + 2 * ATTN_WIDTH].astype(bf16)
    w_t = jnp.concatenate([w_in[:, c0:c0 + ATTN_WIDTH], w_in[:, c0 + 2 * ATTN_WIDTH:],
                           jnp.zeros((D_MODEL, F_ROWS - N_HEADS), w_in.dtype)], axis=1).T.astype(bf16)
    b_ft = jnp.pad(b_forget.astype(f32), (0, F_ROWS - N_HEADS)).reshape(F_ROWS, 1)
    q_gain = (jnp.tile(q_norm, N_HEADS).astype(f32) * (LOG2E / math.sqrt(HEAD_DIM))).reshape(ATTN_WIDTH, 1)
    k_gain = row(jnp.tile(k_norm, N_HEADS))
    head_id = jnp.arange(ATTN_WIDTH) // HEAD_DIM
    headsum = (head_id[:, None] == head_id[None, :]).astype(bf16)
    triu = (jnp.arange(TOKEN_TILE)[:, None] <= jnp.arange(TOKEN_TILE)[None, :]).astype(bf16)
    selk, ck = _k_aug_constants()

    seq_spec = lambda width: pl.BlockSpec((1, TOKEN_TILE, width), lambda b, s: (b, s, 0))
    y_pool, qt, qaugt, ka, vt = pl.pallas_call(
        _in_proj_kernel,
        grid=(B, S // TOKEN_TILE),
        in_specs=[seq_spec(D_MODEL), _resident((1, D_MODEL)),
                  _resident((D_MODEL, POOL_WIDTH)), _resident((D_MODEL, ATTN_WIDTH)),
                  _resident((2 * ATTN_WIDTH + F_ROWS, D_MODEL)), _resident((F_ROWS, 1)),
                  _resident((POOL_GROUPS, POOL_GROUP_DIM, POOL_GROUP_DIM)), _resident((1, POOL_WIDTH)),
                  _resident((1, POOL_WIDTH)), _resident((ATTN_WIDTH, 1)), _resident((1, ATTN_WIDTH)),
                  _resident((ATTN_WIDTH, ATTN_WIDTH)), _resident((TOKEN_TILE, TOKEN_TILE)),
                  _resident((LANES, N_HEAD_BLOCKS * LANES)), _resident((1, N_HEAD_BLOCKS * LANES))],
        out_specs=[seq_spec(POOL_WIDTH),
                   pl.BlockSpec((1, ATTN_WIDTH, TOKEN_TILE), lambda b, s: (b, 0, s)),
                   pl.BlockSpec((1, N_HEADS, BF16_ROWS, TOKEN_TILE), lambda b, s: (b, 0, 0, s)),
                   seq_spec(N_HEAD_BLOCKS * QK_DEPTH),
                   pl.BlockSpec((1, ATTN_WIDTH, TOKEN_TILE), lambda b, s: (b, 0, s))],
        out_shape=[jax.ShapeDtypeStruct((B, S, POOL_WIDTH), bf16),
                   jax.ShapeDtypeStruct((B, ATTN_WIDTH, S), bf16),
                   jax.ShapeDtypeStruct((B, N_HEADS, BF16_ROWS, S), bf16),
                   jax.ShapeDtypeStruct((B, S, N_HEAD_BLOCKS * QK_DEPTH), bf16),
                   jax.ShapeDtypeStruct((B, ATTN_WIDTH, S), bf16)],
        scratch_shapes=[pltpu.VMEM((POOL_HALO, POOL_WIDTH), f32), pltpu.VMEM((F_ROWS, LANES), f32)],
        compiler_params=params_2d,
        name="in_proj",
    )(x1.reshape(B, S, D_MODEL), row(mix_norm), w_pv, w_k, w_t, b_ft,
      pool_w.astype(bf16), row(pool_scale), row(out_norm_pool), q_gain, k_gain, headsum, triu, selk, ck)

    assert S == ATTN_TILE * sum(len(tiles) for tiles in ATTN_TILE_GROUPS)
    seq_rows = lambda width: pl.BlockSpec((1, S, width), lambda b, g: (b, 0, 0))
    seq_lanes = lambda rows: pl.BlockSpec((1, rows, S), lambda b, g: (b, 0, 0))
    o_attn = pl.pallas_call(
        _attn_kernel,
        grid=(B, len(ATTN_TILE_GROUPS)),
        in_specs=[seq_lanes(ATTN_WIDTH), pl.BlockSpec((1, N_HEADS, BF16_ROWS, S), lambda b, g: (b, 0, 0, 0)),
                  seq_rows(N_HEAD_BLOCKS * QK_DEPTH), seq_lanes(ATTN_WIDTH)],
        out_specs=seq_rows(ATTN_WIDTH),
        out_shape=jax.ShapeDtypeStruct((B, S, ATTN_WIDTH), f32),
        scratch_shapes=[pltpu.VMEM((N_HEADS, LANES, S), bf16),
                        pltpu.VMEM((max(len(tiles) for tiles in ATTN_TILE_GROUPS), N_HEADS, QK_DEPTH, ATTN_TILE),
                                   bf16),
                        pltpu.VMEM((2, 2, N_HEADS, ATTN_TILE, ATTN_TILE), f32),
                        pltpu.VMEM((2, N_HEADS, 1, ATTN_TILE), f32),
                        pltpu.VMEM((2, N_HEADS, LANES, ATTN_TILE), f32)],
        compiler_params=params_2d,
        name="attention",
    )(qt, qaugt, ka, vt)

    half_spec = pl.BlockSpec((ffn_tile, POOL_WIDTH), lambda t: (t, 0))
    out = pl.pallas_call(
        _mix_ffn2_kernel,
        grid=(n_tok,),
        in_specs=[tok_spec, half_spec, half_spec, _resident((1, ATTN_WIDTH)),
                  _resident((D_MODEL, D_MODEL))] + _ffn_weight_specs(),
        out_specs=tok_spec,
        out_shape=jax.ShapeDtypeStruct((T, D_MODEL), f32),
        compiler_params=params_1d,
        name="mix_ffn2",
    )(x1, y_pool.reshape(T, POOL_WIDTH), o_attn.reshape(T, ATTN_WIDTH), row(out_norm_attn),
      w_out_bf, row(ffn2_norm), w_gate2, w_up2, w_down2)
    return out.reshape(B, S, D_MODEL)
```

```python
import math

import numpy as np
import jax
import jax.numpy as jnp
from jax import lax
from jax.experimental import pallas as pl
from jax.experimental.pallas import tpu as pltpu

D_MODEL = 1024
D_FF = 2816
POOL_WIDTH = 512
POOL_GROUPS = 4
POOL_GROUP_DIM = 128
POOL_WINDOWS = (2, 4, 8, 16)
ATTN_WIDTH = 512
HEAD_DIM = 64
N_HEADS = 8
EPS = 1e-6
LOG2E = math.log2(math.e)

LANES = 128
BF16_ROWS = 16
F_ROWS = BF16_ROWS
HEADS_PER_BLOCK = LANES // HEAD_DIM
N_HEAD_BLOCKS = N_HEADS // HEADS_PER_BLOCK
POOL_HALO = 16
N_SPLIT = 3
AUG_STRIDE = 2 * N_SPLIT
QK_DEPTH = 2 * LANES

TOKEN_TILE = 512
FFN_PASSES = 2
FF_CHUNK = 256
ATTN_TILE = 256
ATTN_TILE_GROUPS = ((7, 0, 6, 1), (5, 2, 4, 3))
VMEM_LIMIT = 56 * 1024 * 1024
MASK_VALUE = -1e30

bf16 = jnp.bfloat16
f32 = jnp.float32


def _rms_scale(x):
    return lax.rsqrt(jnp.mean(x * x, axis=-1, keepdims=True) + EPS)


def _swiglu_residual(x, norm_w, wg_ref, wu_ref, wd_ref):
    return _swiglu_residual_batch([x], norm_w, wg_ref, wu_ref, wd_ref)[0]


def _swiglu_residual_batch(xs, norm_w, wg_ref, wu_ref, wd_ref):
    hs = [(x * norm_w).astype(bf16) for x in xs]
    rs = [_rms_scale(x) for x in xs]
    accs = [jnp.zeros(x.shape, f32) for x in xs]
    for c in range(D_FF // FF_CHUNK):
        cols = slice(c * FF_CHUNK, (c + 1) * FF_CHUNK)
        acts = []
        for h, r in zip(hs, rs):
            g = r * jnp.dot(h, wg_ref[:, cols], preferred_element_type=f32)
            u = jnp.dot(h, wu_ref[:, cols], preferred_element_type=f32)
            acts.append((g * jax.nn.sigmoid(g) * u).astype(bf16))
        accs = [acc + jnp.dot(a, wd_ref[cols, :], preferred_element_type=f32) for acc, a in zip(accs, acts)]
    return [x + (0.5 * r) * acc for x, r, acc in zip(xs, rs, accs)]


def _ffn1_kernel(x_ref, nw_ref, wg_ref, wu_ref, wd_ref, g2_ref, u2_ref, d2_ref, wo_ref,
                 o_ref, g2_bf_ref, u2_bf_ref, d2_bf_ref, wo_bf_ref):
    for src_ref, dst_ref in ((g2_ref, g2_bf_ref), (u2_ref, u2_bf_ref), (wo_ref, wo_bf_ref)):
        dst_ref[...] = src_ref[...].astype(bf16)

    @pl.when(pl.program_id(0) < D_FF // d2_ref.shape[0])
    def _():
        d2_bf_ref[...] = d2_ref[...].astype(bf16)

    tiles = [slice(s * TOKEN_TILE, (s + 1) * TOKEN_TILE) for s in range(FFN_PASSES)]
    outs = _swiglu_residual_batch([x_ref[rows, :] for rows in tiles], nw_ref[...], wg_ref, wu_ref, wd_ref)
    for rows, out in zip(tiles, outs):
        o_ref[rows, :] = out


def _mix_ffn2_kernel(x_ref, yp_ref, oa_ref, nattn_ref, wo_ref, nw_ref, wg_ref, wu_ref, wd_ref, o_ref):
    tiles = [slice(s * TOKEN_TILE, (s + 1) * TOKEN_TILE) for s in range(FFN_PASSES)]
    x2s = []
    for rows in tiles:
        oa = oa_ref[rows, :]
        ya = (oa * _rms_scale(oa) * nattn_ref[...]).astype(bf16)
        x2s.append(x_ref[rows, :]
                   + jnp.dot(yp_ref[rows, :], wo_ref[:POOL_WIDTH, :], preferred_element_type=f32)
                   + jnp.dot(ya, wo_ref[POOL_WIDTH:, :], preferred_element_type=f32))
    for rows, out in zip(tiles, _swiglu_residual_batch(x2s, nw_ref[...], wg_ref, wu_ref, wd_ref)):
        o_ref[rows, :] = out


def _split3(x):
    hi = x.astype(bf16).astype(f32)
    r = x - hi
    mid = r.astype(bf16).astype(f32)
    lo = (r - mid).astype(bf16).astype(f32)
    return hi, mid, lo


def _in_proj_kernel(x_ref, nw_ref, wpv_ref, wk_ref, wt_ref, bft_ref,
                    poolw_ref, pscale_ref, npool_ref, qn_ref, kn_ref, headsum_ref, triu_ref,
                    selk_ref, ck_ref,
                    yp_ref, qt_ref, qaugt_ref, ka_ref, vt_ref,
                    halo_ref, carry_ref):
    s_idx = pl.program_id(1)
    tm = x_ref.shape[1]

    @pl.when(s_idx == 0)
    def _():
        halo_ref[...] = jnp.zeros(halo_ref.shape, f32)
        carry_ref[...] = jnp.zeros(carry_ref.shape, f32)

    x = x_ref[0]
    h = (x * nw_ref[...]).astype(bf16)
    r = _rms_scale(x)
    r_row = jnp.broadcast_to(r, (tm, LANES)).T[0:1]

    qvf = r_row * lax.dot_general(wt_ref[...], h, (((1,), (1,)), ((), ())), preferred_element_type=f32)
    pv = r * jnp.dot(h, wpv_ref[...], preferred_element_type=f32)
    t = r * jnp.dot(h, wk_ref[...], preferred_element_type=f32)

    fl = qvf[2 * ATTN_WIDTH:] + bft_ref[...]
    logf = (jnp.minimum(fl, 0.0) - jnp.log1p(jnp.exp(-jnp.abs(fl)))) * LOG2E
    logf = jnp.where(lax.broadcasted_iota(jnp.int32, logf.shape, 0) < N_HEADS, logf, 0.0)
    parts = jnp.concatenate(_split3(logf), axis=0).astype(bf16)
    sums = jnp.dot(parts, triu_ref[...], preferred_element_type=f32)
    fcum_t = (sums[:F_ROWS] + sums[F_ROWS:2 * F_ROWS] + sums[2 * F_ROWS:]) + carry_ref[:, 0:1]
    carry_ref[...] = jnp.broadcast_to(fcum_t[:, tm - 1:tm], carry_ref.shape)
    hi_t, mid_t, lo_t = (part[:N_HEADS] for part in _split3(fcum_t))

    full = jnp.concatenate([halo_ref[...], pv], axis=0)
    halo_ref[...] = pv[tm - POOL_HALO:, :]
    pos = s_idx * tm + lax.broadcasted_iota(jnp.int32, (tm, 1), 0)
    mixed = []
    for g in range(POOL_GROUPS):
        w = POOL_WINDOWS[g]
        cols = slice(g * POOL_GROUP_DIM, (g + 1) * POOL_GROUP_DIM)
        win = full[:, cols]
        shift = 1
        while shift < w:
            win = win + pltpu.roll(win, shift, 0)
            shift *= 2
        count = jnp.minimum(pos + 1, w).astype(f32)
        pooled = win[POOL_HALO:, :] / count - pv[:, cols]
        mixed.append(jnp.dot(pooled.astype(bf16), poolw_ref[g], preferred_element_type=f32))
    mixed = jnp.concatenate(mixed, axis=1) * pscale_ref[...]
    yp_ref[0] = (mixed * _rms_scale(mixed) * npool_ref[...]).astype(bf16)

    ssq = jnp.dot((t * t).astype(bf16), headsum_ref[...], preferred_element_type=f32)
    kn = (t * lax.rsqrt(ssq * (1.0 / HEAD_DIM) + EPS) * kn_ref[...]).astype(bf16)

    packed_t = jnp.concatenate([hi_t, mid_t, lo_t, jnp.zeros((LANES - N_SPLIT * N_HEADS, tm), f32)], axis=0)
    packed = packed_t.T.astype(bf16)
    kaug = (jnp.dot(packed, selk_ref[...], preferred_element_type=f32) + ck_ref[...]).astype(bf16)
    pieces = []
    for p in range(N_HEAD_BLOCKS):
        cols = slice(p * LANES, (p + 1) * LANES)
        pieces += [kn[:, cols], kaug[:, cols]]
    ka_ref[0] = jnp.concatenate(pieces, axis=1)

    qt = qvf[:ATTN_WIDTH].reshape(N_HEADS, HEAD_DIM, tm)
    qt = qt * lax.rsqrt(jnp.mean(qt * qt, axis=1, keepdims=True) + EPS)
    qt_ref[0] = (qt.reshape(ATTN_WIDTH, tm) * qn_ref[...]).astype(bf16)
    vt_ref[0] = qvf[ATTN_WIDTH:2 * ATTN_WIDTH].astype(bf16)

    rowid = lax.broadcasted_iota(jnp.int32, (BF16_ROWS, tm), 0)
    for hh in range(N_HEADS):
        base = (hh % HEADS_PER_BLOCK) * AUG_STRIDE
        ones_rows = ((rowid >= base + N_SPLIT) & (rowid < base + 2 * N_SPLIT)).astype(f32)
        aug = jnp.where(rowid == base, hi_t[hh:hh + 1],
                        jnp.where(rowid == base + 1, mid_t[hh:hh + 1],
                                  jnp.where(rowid == base + 2, lo_t[hh:hh + 1], ones_rows)))
        qaugt_ref[0, hh] = aug.astype(bf16)


def _attn_kernel(qt_ref, qaugt_ref, ka_ref, vt_ref, o_ref, vth_ref, qat_ref, s_ref, m_ref, acc_ref):
    tq = qat_ref.shape[3]
    tk = tq
    n_tiles = ka_ref.shape[1] // tk

    def build_values_operand(hh):
        seq = vt_ref.shape[2]
        ones_then_zeros = (lax.broadcasted_iota(jnp.int32, (HEAD_DIM, seq), 0) == 0).astype(bf16)
        vh = vt_ref[0, hh * HEAD_DIM:(hh + 1) * HEAD_DIM, :]
        odd = hh % HEADS_PER_BLOCK
        vth_ref[hh, 0:HEAD_DIM, :] = ones_then_zeros if odd else vh
        vth_ref[hh, HEAD_DIM:LANES, :] = vh if odd else ones_then_zeros

    def build_query_operand(pos, i, hh):
        queries = slice(i * tq, (i + 1) * tq)
        zeros_head = jnp.zeros((HEAD_DIM, tq), bf16)
        qh = qt_ref[0, hh * HEAD_DIM:(hh + 1) * HEAD_DIM, queries]
        odd = hh % HEADS_PER_BLOCK
        qat_ref[pos, hh, 0:HEAD_DIM, :] = zeros_head if odd else qh
        qat_ref[pos, hh, HEAD_DIM:LANES, :] = qh if odd else zeros_head
        qat_ref[pos, hh, LANES:LANES + BF16_ROWS, :] = qaugt_ref[0, hh, :, queries]
        qat_ref[pos, hh, LANES + BF16_ROWS:, :] = jnp.zeros((QK_DEPTH - LANES - BF16_ROWS, tq), bf16)

    def keys(j):
        return slice(j * tk, (j + 1) * tk)

    def logits(pos, i, j, hh):
        w = pos % 2
        p = hh // HEADS_PER_BLOCK
        ka = ka_ref[0, keys(j), p * QK_DEPTH:(p + 1) * QK_DEPTH]
        st = jnp.dot(ka, qat_ref[pos, hh], preferred_element_type=f32)
        if j == i:
            key_minus_query = (lax.broadcasted_iota(jnp.int32, (tk, tq), 0)
                               - lax.broadcasted_iota(jnp.int32, (tk, tq), 1))
            st = jnp.where(key_minus_query > 0, MASK_VALUE, st)
        s_ref[w, j % 2, hh] = st

    def softmax_values(w, j, hh):
        st_ref = s_ref.at[w, j % 2, hh]
        vt = vth_ref[hh, :, keys(j)]
        m_blk = jnp.max(st_ref[...], axis=0, keepdims=True)
        if j == 0:
            m_ref[w, hh] = m_blk
            pt = jnp.exp2(st_ref[...] - m_blk).astype(bf16)
            acc_ref[w, hh] = jnp.dot(vt, pt, preferred_element_type=f32)
        else:
            m_prev = m_ref[w, hh]
            m_new = jnp.maximum(m_prev, m_blk)
            m_ref[w, hh] = m_new
            pt = jnp.exp2(st_ref[...] - m_new).astype(bf16)
            acc_ref[w, hh] = (jnp.exp2(m_prev - m_new) * acc_ref[w, hh]
                              + jnp.dot(vt, pt, preferred_element_type=f32))

    def write_output(w, i, p):
        row = lax.broadcasted_iota(jnp.int32, (LANES, tq), 0)
        even, odd = acc_ref[w, 2 * p], acc_ref[w, 2 * p + 1]
        ot = jnp.where(row < HEAD_DIM, even / even[HEAD_DIM:HEAD_DIM + 1, :], odd / odd[0:1, :])
        o_ref[0, i * tq:(i + 1) * tq, p * LANES:(p + 1) * LANES] = ot.T

    def run(tiles, first_group):
        def start(pos):
            for hh in range(N_HEADS):
                build_query_operand(pos, tiles[pos], hh)
                logits(pos, tiles[pos], 0, hh)

        def key_block(pos, j):
            w, i = pos % 2, tiles[pos]
            for hh in range(N_HEADS):
                if first_group and pos == 0 and j == 0:
                    build_values_operand(hh)
                softmax_values(w, j, hh)
                if j < i:
                    logits(pos, i, j + 1, hh)
                elif hh % HEADS_PER_BLOCK == HEADS_PER_BLOCK - 1:
                    write_output(w, i, hh // HEADS_PER_BLOCK)

        start(0)
        for pos, i in enumerate(tiles):
            for j in range(i + 1):
                if j == i and pos + 1 < len(tiles):
                    start(pos + 1)
                key_block(pos, j)

    assert sorted(i for tiles in ATTN_TILE_GROUPS for i in tiles) == list(range(n_tiles))
    for gi, tiles in enumerate(ATTN_TILE_GROUPS):
        @pl.when(pl.program_id(1) == gi)
        def _(tiles=tiles, first_group=(gi == 0)):
            run(tiles, first_group)


def _resident(shape):
    nd = len(shape)
    return pl.BlockSpec(shape, lambda *_: (0,) * nd, pipeline_mode=pl.Buffered(1))


def _ffn_weight_specs():
    return [_resident((1, D_MODEL)), _resident((D_MODEL, D_FF)), _resident((D_MODEL, D_FF)),
            _resident((D_FF, D_MODEL))]


def _k_aug_constants():
    selk = np.zeros((LANES, N_HEAD_BLOCKS * LANES), np.float32)
    ck = np.zeros((1, N_HEAD_BLOCKS * LANES), np.float32)
    for hh in range(N_HEADS):
        p, odd = divmod(hh, HEADS_PER_BLOCK)
        base = odd * AUG_STRIDE
        for part in range(N_SPLIT):
            ck[0, p * LANES + base + part] = 1.0
            selk[part * N_HEADS + hh, p * LANES + base + N_SPLIT + part] = -1.0
    return jnp.asarray(selk, bf16), jnp.asarray(ck)


def kernel(x, ffn1_norm, ffn1_w_gate, ffn1_w_up, ffn1_w_down, mix_norm, w_in, b_forget, pool_w, pool_scale, q_norm, k_norm, out_norm_pool, out_norm_attn, w_out, ffn2_norm, ffn2_w_gate, ffn2_w_up, ffn2_w_down):
    B, S, D = x.shape
    ffn_tile = FFN_PASSES * TOKEN_TILE
    assert D == D_MODEL and S % ffn_tile == 0 and S % ATTN_TILE == 0
    T = B * S
    n_tok = T // ffn_tile
    row = lambda a: a.reshape(1, -1).astype(f32)

    tok_spec = pl.BlockSpec((ffn_tile, D_MODEL), lambda t: (t, 0))
    params_1d = pltpu.CompilerParams(dimension_semantics=("arbitrary",), vmem_limit_bytes=VMEM_LIMIT)
    params_2d = pltpu.CompilerParams(dimension_semantics=("arbitrary", "arbitrary"), vmem_limit_bytes=VMEM_LIMIT)

    assert D_MODEL % n_tok == 0 and (D_MODEL // n_tok) % BF16_ROWS == 0
    slab = D_MODEL // n_tok
    down_slab = BF16_ROWS * (D_FF // FF_CHUNK)
    n_down = D_FF // down_slab
    assert n_down * down_slab == D_FF and n_down <= n_tok
    ff_slab = pl.BlockSpec((slab, D_FF), lambda t: (t, 0))
    wd_slab = pl.BlockSpec((down_slab, D_MODEL), lambda t: (jnp.minimum(t, n_down - 1), 0))
    wo_slab = pl.BlockSpec((slab, D_MODEL), lambda t: (t, 0))
    x1, w_gate2, w_up2, w_down2, w_out_bf = pl.pallas_call(
        _ffn1_kernel,
        grid=(n_tok,),
        in_specs=[tok_spec] + _ffn_weight_specs() + [ff_slab, ff_slab, wd_slab, wo_slab],
        out_specs=[tok_spec, ff_slab, ff_slab, wd_slab, wo_slab],
        out_shape=[jax.ShapeDtypeStruct((T, D_MODEL), f32),
                   jax.ShapeDtypeStruct((D_MODEL, D_FF), bf16), jax.ShapeDtypeStruct((D_MODEL, D_FF), bf16),
                   jax.ShapeDtypeStruct((D_FF, D_MODEL), bf16), jax.ShapeDtypeStruct((D_MODEL, D_MODEL), bf16)],
        compiler_params=params_1d,
        name="ffn1",
    )(x.reshape(T, D_MODEL), row(ffn1_norm), ffn1_w_gate.astype(bf16), ffn1_w_up.astype(bf16),
      ffn1_w_down.astype(bf16), ffn2_w_gate, ffn2_w_up, ffn2_w_down, w_out)

    c0 = POOL_WIDTH
    w_pv = w_in[:, :c0].astype(bf16)
    w_k = w_in[:, c0 + ATTN_WIDTH:c0 + 2 * ATTN_WIDTH].astype(bf16)
    w_t = jnp.concatenate([w_in[:, c0:c0 + ATTN_WIDTH], w_in[:, c0 + 2 * ATTN_WIDTH:],
                           jnp.zeros((D_MODEL, F_ROWS - N_HEADS), w_in.dtype)], axis=1).T.astype(bf16)
    b_ft = jnp.pad(b_forget.astype(f32), (0, F_ROWS - N_HEADS)).reshape(F_ROWS, 1)
    q_gain = (jnp.tile(q_norm, N_HEADS).astype(f32) * (LOG2E / math.sqrt(HEAD_DIM))).reshape(ATTN_WIDTH, 1)
    k_gain = row(jnp.tile(k_norm, N_HEADS))
    head_id = jnp.arange(ATTN_WIDTH) // HEAD_DIM
    headsum = (head_id[:, None] == head_id[None, :]).astype(bf16)
    triu = (jnp.arange(TOKEN_TILE)[:, None] <= jnp.arange(TOKEN_TILE)[None, :]).astype(bf16)
    selk, ck = _k_aug_constants()

    seq_spec = lambda width: pl.BlockSpec((1, TOKEN_TILE, width), lambda b, s: (b, s, 0))
    y_pool, qt, qaugt, ka, vt = pl.pallas_call(
        _in_proj_kernel,
        grid=(B, S // TOKEN_TILE),
        in_specs=[seq_spec(D_MODEL), _resident((1, D_MODEL)),
                  _resident((D_MODEL, POOL_WIDTH)), _resident((D_MODEL, ATTN_WIDTH)),
                  _resident((2 * ATTN_WIDTH + F_ROWS, D_MODEL)), _resident((F_ROWS, 1)),
                  _resident((POOL_GROUPS, POOL_GROUP_DIM, POOL_GROUP_DIM)), _resident((1, POOL_WIDTH)),
                  _resident((1, POOL_WIDTH)), _resident((ATTN_WIDTH, 1)), _resident((1, ATTN_WIDTH)),
                  _resident((ATTN_WIDTH, ATTN_WIDTH)), _resident((TOKEN_TILE, TOKEN_TILE)),
                  _resident((LANES, N_HEAD_BLOCKS * LANES)), _resident((1, N_HEAD_BLOCKS * LANES))],
        out_specs=[seq_spec(POOL_WIDTH),
                   pl.BlockSpec((1, ATTN_WIDTH, TOKEN_TILE), lambda b, s: (b, 0, s)),
                   pl.BlockSpec((1, N_HEADS, BF16_ROWS, TOKEN_TILE), lambda b, s: (b, 0, 0, s)),
                   seq_spec(N_HEAD_BLOCKS * QK_DEPTH),
                   pl.BlockSpec((1, ATTN_WIDTH, TOKEN_TILE), lambda b, s: (b, 0, s))],
        out_shape=[jax.ShapeDtypeStruct((B, S, POOL_WIDTH), bf16),
                   jax.ShapeDtypeStruct((B, ATTN_WIDTH, S), bf16),
                   jax.ShapeDtypeStruct((B, N_HEADS, BF16_ROWS, S), bf16),
                   jax.ShapeDtypeStruct((B, S, N_HEAD_BLOCKS * QK_DEPTH), bf16),
                   jax.ShapeDtypeStruct((B, ATTN_WIDTH, S), bf16)],
        scratch_shapes=[pltpu.VMEM((POOL_HALO, POOL_WIDTH), f32), pltpu.VMEM((F_ROWS, LANES), f32)],
        compiler_params=params_2d,
        name="in_proj",
    )(x1.reshape(B, S, D_MODEL), row(mix_norm), w_pv, w_k, w_t, b_ft,
      pool_w.astype(bf16), row(pool_scale), row(out_norm_pool), q_gain, k_gain, headsum, triu, selk, ck)

    assert S == ATTN_TILE * sum(len(tiles) for tiles in ATTN_TILE_GROUPS)
    seq_rows = lambda width: pl.BlockSpec((1, S, width), lambda b, g: (b, 0, 0))
    seq_lanes = lambda rows: pl.BlockSpec((1, rows, S), lambda b, g: (b, 0, 0))
    o_attn = pl.pallas_call(
        _attn_kernel,
        grid=(B, len(ATTN_TILE_GROUPS)),
        in_specs=[seq_lanes(ATTN_WIDTH), pl.BlockSpec((1, N_HEADS, BF16_ROWS, S), lambda b, g: (b, 0, 0, 0)),
                  seq_rows(N_HEAD_BLOCKS * QK_DEPTH), seq_lanes(ATTN_WIDTH)],
        out_specs=seq_rows(ATTN_WIDTH),
        out_shape=jax.ShapeDtypeStruct((B, S, ATTN_WIDTH), f32),
        scratch_shapes=[pltpu.VMEM((N_HEADS, LANES, S), bf16),
                        pltpu.VMEM((max(len(tiles) for tiles in ATTN_TILE_GROUPS), N_HEADS, QK_DEPTH, ATTN_TILE),
                                   bf16),
                        pltpu.VMEM((2, 2, N_HEADS, ATTN_TILE, ATTN_TILE), f32),
                        pltpu.VMEM((2, N_HEADS, 1, ATTN_TILE), f32),
                        pltpu.VMEM((2, N_HEADS, LANES, ATTN_TILE), f32)],
        compiler_params=params_2d,
        name="attention",
    )(qt, qaugt, ka, vt)

    half_spec = pl.BlockSpec((ffn_tile, POOL_WIDTH), lambda t: (t, 0))
    out = pl.pallas_call(
        _mix_ffn2_kernel,
        grid=(n_tok,),
        in_specs=[tok_spec, half_spec, half_spec, _resident((1, ATTN_WIDTH)),
                  _resident((D_MODEL, D_MODEL))] + _ffn_weight_specs(),
        out_specs=tok_spec,
        out_shape=jax.ShapeDtypeStruct((T, D_MODEL), f32),
        compiler_params=params_1d,
        name="mix_ffn2",
    )(x1, y_pool.reshape(T, POOL_WIDTH), o_attn.reshape(T, ATTN_WIDTH), row(out_norm_attn),
      w_out_bf, row(ffn2_norm), w_gate2, w_up2, w_down2)
    return out.reshape(B, S, D_MODEL)
```

```python
import math

import numpy as np
import jax
import jax.numpy as jnp
from jax import lax
from jax.experimental import pallas as pl
from jax.experimental.pallas import tpu as pltpu

D_MODEL = 1024
D_FF = 2816
POOL_WIDTH = 512
POOL_GROUPS = 4
POOL_GROUP_DIM = 128
POOL_WINDOWS = (2, 4, 8, 16)
ATTN_WIDTH = 512
HEAD_DIM = 64
N_HEADS = 8
EPS = 1e-6
LOG2E = math.log2(math.e)

LANES = 128
BF16_ROWS = 16
F_ROWS = BF16_ROWS
HEADS_PER_BLOCK = LANES // HEAD_DIM
N_HEAD_BLOCKS = N_HEADS // HEADS_PER_BLOCK
POOL_HALO = 16
N_SPLIT = 3
AUG_STRIDE = 2 * N_SPLIT
QK_DEPTH = 2 * LANES

TOKEN_TILE = 512
FFN_PASSES = 2
FF_CHUNK = 256
ATTN_TILE = 256
ATTN_TILE_GROUPS = ((7, 0, 6, 1), (5, 2, 4, 3))
VMEM_LIMIT = 56 * 1024 * 1024
MASK_VALUE = -1e30

bf16 = jnp.bfloat16
f32 = jnp.float32


def _rms_scale(x):
    return lax.rsqrt(jnp.mean(x * x, axis=-1, keepdims=True) + EPS)


def _swiglu_residual(x, norm_w, wg_ref, wu_ref, wd_ref):
    h = (x * norm_w).astype(bf16)
    r = _rms_scale(x)
    acc = jnp.zeros(x.shape, f32)
    for c in range(D_FF // FF_CHUNK):
        cols = slice(c * FF_CHUNK, (c + 1) * FF_CHUNK)
        g = r * jnp.dot(h, wg_ref[:, cols], preferred_element_type=f32)
        u = jnp.dot(h, wu_ref[:, cols], preferred_element_type=f32)
        a = (g * jax.nn.sigmoid(g) * u).astype(bf16)
        acc = acc + jnp.dot(a, wd_ref[cols, :], preferred_element_type=f32)
    return x + (0.5 * r) * acc


def _ffn1_kernel(x_ref, nw_ref, wg_ref, wu_ref, wd_ref, g2_ref, u2_ref, d2_ref, wo_ref,
                 o_ref, g2_bf_ref, u2_bf_ref, d2_bf_ref, wo_bf_ref):
    for src_ref, dst_ref in ((g2_ref, g2_bf_ref), (u2_ref, u2_bf_ref), (wo_ref, wo_bf_ref)):
        dst_ref[...] = src_ref[...].astype(bf16)

    @pl.when(pl.program_id(0) < D_FF // d2_ref.shape[0])
    def _():
        d2_bf_ref[...] = d2_ref[...].astype(bf16)

    for s in range(FFN_PASSES):
        rows = slice(s * TOKEN_TILE, (s + 1) * TOKEN_TILE)
        o_ref[rows, :] = _swiglu_residual(x_ref[rows, :], nw_ref[...], wg_ref, wu_ref, wd_ref)


def _mix_ffn2_kernel(x_ref, yp_ref, oa_ref, nattn_ref, wo_ref, nw_ref, wg_ref, wu_ref, wd_ref, o_ref):
    for s in range(FFN_PASSES):
        rows = slice(s * TOKEN_TILE, (s + 1) * TOKEN_TILE)
        oa = oa_ref[rows, :]
        ya = (oa * nattn_ref[...]).astype(bf16)
        x2 = (x_ref[rows, :]
              + jnp.dot(yp_ref[rows, :], wo_ref[:POOL_WIDTH, :], preferred_element_type=f32)
              + _rms_scale(oa) * jnp.dot(ya, wo_ref[POOL_WIDTH:, :], preferred_element_type=f32))
        o_ref[rows, :] = _swiglu_residual(x2, nw_ref[...], wg_ref, wu_ref, wd_ref)


def _split3(x):
    hi = x.astype(bf16).astype(f32)
    r = x - hi
    mid = r.astype(bf16).astype(f32)
    lo = (r - mid).astype(bf16).astype(f32)
    return hi, mid, lo


def _in_proj_kernel(x_ref, nw_ref, wpv_ref, wk_ref, wt_ref, bft_ref,
                    poolw_ref, pscale_ref, npool_ref, qn_ref, kn_ref, headsum_ref, triu_ref,
                    selk_ref, ck_ref,
                    yp_ref, qt_ref, qaugt_ref, ka_ref, vt_ref,
                    halo_ref, carry_ref):
    s_idx = pl.program_id(1)
    tm = x_ref.shape[1]

    @pl.when(s_idx == 0)
    def _():
        halo_ref[...] = jnp.zeros(halo_ref.shape, f32)
        carry_ref[...] = jnp.zeros(carry_ref.shape, f32)

    x = x_ref[0]
    h = (x * nw_ref[...]).astype(bf16)
    r = _rms_scale(x)
    r_row = jnp.broadcast_to(r, (tm, LANES)).T[0:1]

    qvf = r_row * lax.dot_general(wt_ref[...], h, (((1,), (1,)), ((), ())), preferred_element_type=f32)
    pv = r * jnp.dot(h, wpv_ref[...], preferred_element_type=f32)
    t = r * jnp.dot(h, wk_ref[...], preferred_element_type=f32)

    fl = qvf[2 * ATTN_WIDTH:] + bft_ref[...]
    logf = (jnp.minimum(fl, 0.0) - jnp.log1p(jnp.exp(-jnp.abs(fl)))) * LOG2E
    logf = jnp.where(lax.broadcasted_iota(jnp.int32, logf.shape, 0) < N_HEADS, logf, 0.0)
    parts = jnp.concatenate(_split3(logf), axis=0).astype(bf16)
    sums = jnp.dot(parts, triu_ref[...], preferred_element_type=f32)
    fcum_t = (sums[:F_ROWS] + sums[F_ROWS:2 * F_ROWS] + sums[2 * F_ROWS:]) + carry_ref[:, 0:1]
    carry_ref[...] = jnp.broadcast_to(fcum_t[:, tm - 1:tm], carry_ref.shape)
    hi_t, mid_t, lo_t = (part[:N_HEADS] for part in _split3(fcum_t))

    full = jnp.concatenate([halo_ref[...], pv], axis=0)
    halo_ref[...] = pv[tm - POOL_HALO:, :]
    pos = s_idx * tm + lax.broadcasted_iota(jnp.int32, (tm, 1), 0)
    mixed = []
    for g in range(POOL_GROUPS):
        w = POOL_WINDOWS[g]
        cols = slice(g * POOL_GROUP_DIM, (g + 1) * POOL_GROUP_DIM)
        win = full[:, cols]
        shift = 1
        while shift < w:
            win = win + pltpu.roll(win, shift, 0)
            shift *= 2
        count = jnp.minimum(pos + 1, w).astype(f32)
        pooled = win[POOL_HALO:, :] / count - pv[:, cols]
        mixed.append(jnp.dot(pooled.astype(bf16), poolw_ref[g], preferred_element_type=f32))
    mixed = jnp.concatenate(mixed, axis=1) * pscale_ref[...]
    yp_ref[0] = (mixed * _rms_scale(mixed) * npool_ref[...]).astype(bf16)

    ssq = jnp.dot((t * t).astype(bf16), headsum_ref[...], preferred_element_type=f32)
    kn = (t * lax.rsqrt(ssq * (1.0 / HEAD_DIM) + EPS) * kn_ref[...]).astype(bf16)

    packed_t = jnp.concatenate([hi_t, mid_t, lo_t, jnp.zeros((LANES - N_SPLIT * N_HEADS, tm), f32)], axis=0)
    packed = packed_t.T.astype(bf16)
    kaug = (jnp.dot(packed, selk_ref[...], preferred_element_type=f32) + ck_ref[...]).astype(bf16)
    pieces = []
    for p in range(N_HEAD_BLOCKS):
        cols = slice(p * LANES, (p + 1) * LANES)
        pieces += [kn[:, cols], kaug[:, cols]]
    ka_ref[0] = jnp.concatenate(pieces, axis=1)

    qt = qvf[:ATTN_WIDTH].reshape(N_HEADS, HEAD_DIM, tm)
    qt = qt * lax.rsqrt(jnp.mean(qt * qt, axis=1, keepdims=True) + EPS)
    qt_ref[0] = (qt.reshape(ATTN_WIDTH, tm) * qn_ref[...]).astype(bf16)
    vt_ref[0] = qvf[ATTN_WIDTH:2 * ATTN_WIDTH].astype(bf16)

    rowid = lax.broadcasted_iota(jnp.int32, (BF16_ROWS, tm), 0)
    for hh in range(N_HEADS):
        base = (hh % HEADS_PER_BLOCK) * AUG_STRIDE
        ones_rows = ((rowid >= base + N_SPLIT) & (rowid < base + 2 * N_SPLIT)).astype(f32)
        aug = jnp.where(rowid == base, hi_t[hh:hh + 1],
                        jnp.where(rowid == base + 1, mid_t[hh:hh + 1],
                                  jnp.where(rowid == base + 2, lo_t[hh:hh + 1], ones_rows)))
        qaugt_ref[0, hh] = aug.astype(bf16)


def _attn_kernel(qt_ref, qaugt_ref, ka_ref, vt_ref, o_ref, vth_ref, qat_ref, s_ref, m_ref, acc_ref):
    tq = qat_ref.shape[3]
    tk = tq
    n_tiles = ka_ref.shape[1] // tk

    def build_values_operand(hh):
        seq = vt_ref.shape[2]
        ones_then_zeros = (lax.broadcasted_iota(jnp.int32, (HEAD_DIM, seq), 0) == 0).astype(bf16)
        vh = vt_ref[0, hh * HEAD_DIM:(hh + 1) * HEAD_DIM, :]
        odd = hh % HEADS_PER_BLOCK
        vth_ref[hh, 0:HEAD_DIM, :] = ones_then_zeros if odd else vh
        vth_ref[hh, HEAD_DIM:LANES, :] = vh if odd else ones_then_zeros

    def build_query_operand(pos, i, hh):
        queries = slice(i * tq, (i + 1) * tq)
        zeros_head = jnp.zeros((HEAD_DIM, tq), bf16)
        qh = qt_ref[0, hh * HEAD_DIM:(hh + 1) * HEAD_DIM, queries]
        odd = hh % HEADS_PER_BLOCK
        qat_ref[pos, hh, 0:HEAD_DIM, :] = zeros_head if odd else qh
        qat_ref[pos, hh, HEAD_DIM:LANES, :] = qh if odd else zeros_head
        qat_ref[pos, hh, LANES:LANES + BF16_ROWS, :] = qaugt_ref[0, hh, :, queries]
        qat_ref[pos, hh, LANES + BF16_ROWS:, :] = jnp.zeros((QK_DEPTH - LANES - BF16_ROWS, tq), bf16)

    def keys(j):
        return slice(j * tk, (j + 1) * tk)

    def logits(pos, i, j, hh):
        w = pos % 2
        p = hh // HEADS_PER_BLOCK
        ka = ka_ref[0, keys(j), p * QK_DEPTH:(p + 1) * QK_DEPTH]
        st = jnp.dot(ka, qat_ref[pos, hh], preferred_element_type=f32)
        if j == i:
            key_minus_query = (lax.broadcasted_iota(jnp.int32, (tk, tq), 0)
                               - lax.broadcasted_iota(jnp.int32, (tk, tq), 1))
            st = jnp.where(key_minus_query > 0, MASK_VALUE, st)
        s_ref[w, j % 2, hh] = st

    def softmax_values(w, j, hh):
        st_ref = s_ref.at[w, j % 2, hh]
        vt = vth_ref[hh, :, keys(j)]
        m_blk = jnp.max(st_ref[...], axis=0, keepdims=True)
        if j == 0:
            m_ref[w, hh] = m_blk
            pt = jnp.exp2(st_ref[...] - m_blk).astype(bf16)
            acc_ref[w, hh] = jnp.dot(vt, pt, preferred_element_type=f32)
        else:
            m_prev = m_ref[w, hh]
            m_new = jnp.maximum(m_prev, m_blk)
            m_ref[w, hh] = m_new
            pt = jnp.exp2(st_ref[...] - m_new).astype(bf16)
            acc_ref[w, hh] = (jnp.exp2(m_prev - m_new) * acc_ref[w, hh]
                              + jnp.dot(vt, pt, preferred_element_type=f32))

    def write_output(w, i, p):
        row = lax.broadcasted_iota(jnp.int32, (LANES, tq), 0)
        even, odd = acc_ref[w, 2 * p], acc_ref[w, 2 * p + 1]
        ot = jnp.where(row < HEAD_DIM, even / even[HEAD_DIM:HEAD_DIM + 1, :], odd / odd[0:1, :])
        o_ref[0, i * tq:(i + 1) * tq, p * LANES:(p + 1) * LANES] = ot.T

    def run(tiles, first_group):
        def start(pos):
            for hh in range(N_HEADS):
                build_query_operand(pos, tiles[pos], hh)
                logits(pos, tiles[pos], 0, hh)

        def key_block(pos, j):
            w, i = pos % 2, tiles[pos]
            for hh in range(N_HEADS):
                if first_group and pos == 0 and j == 0:
                    build_values_operand(hh)
                softmax_values(w, j, hh)
                if j < i:
                    logits(pos, i, j + 1, hh)
                elif hh % HEADS_PER_BLOCK == HEADS_PER_BLOCK - 1:
                    write_output(w, i, hh // HEADS_PER_BLOCK)

        start(0)
        for pos, i in enumerate(tiles):
            for j in range(i + 1):
                if j == i and pos + 1 < len(tiles):
                    start(pos + 1)
                key_block(pos, j)

    assert sorted(i for tiles in ATTN_TILE_GROUPS for i in tiles) == list(range(n_tiles))
    for gi, tiles in enumerate(ATTN_TILE_GROUPS):
        @pl.when(pl.program_id(1) == gi)
        def _(tiles=tiles, first_group=(gi == 0)):
            run(tiles, first_group)


def _resident(shape):
    nd = len(shape)
    return pl.BlockSpec(shape, lambda *_: (0,) * nd, pipeline_mode=pl.Buffered(1))


def _ffn_weight_specs():
    return [_resident((1, D_MODEL)), _resident((D_MODEL, D_FF)), _resident((D_MODEL, D_FF)),
            _resident((D_FF, D_MODEL))]


def _k_aug_constants():
    selk = np.zeros((LANES, N_HEAD_BLOCKS * LANES), np.float32)
    ck = np.zeros((1, N_HEAD_BLOCKS * LANES), np.float32)
    for hh in range(N_HEADS):
        p, odd = divmod(hh, HEADS_PER_BLOCK)
        base = odd * AUG_STRIDE
        for part in range(N_SPLIT):
            ck[0, p * LANES + base + part] = 1.0
            selk[part * N_HEADS + hh, p * LANES + base + N_SPLIT + part] = -1.0
    return jnp.asarray(selk, bf16), jnp.asarray(ck)


def kernel(x, ffn1_norm, ffn1_w_gate, ffn1_w_up, ffn1_w_down, mix_norm, w_in, b_forget, pool_w, pool_scale, q_norm, k_norm, out_norm_pool, out_norm_attn, w_out, ffn2_norm, ffn2_w_gate, ffn2_w_up, ffn2_w_down):
    B, S, D = x.shape
    ffn_tile = FFN_PASSES * TOKEN_TILE
    assert D == D_MODEL and S % ffn_tile == 0 and S % ATTN_TILE == 0
    T = B * S
    n_tok = T // ffn_tile
    row = lambda a: a.reshape(1, -1).astype(f32)

    tok_spec = pl.BlockSpec((ffn_tile, D_MODEL), lambda t: (t, 0))
    params_1d = pltpu.CompilerParams(dimension_semantics=("arbitrary",), vmem_limit_bytes=VMEM_LIMIT)
    params_2d = pltpu.CompilerParams(dimension_semantics=("arbitrary", "arbitrary"), vmem_limit_bytes=VMEM_LIMIT)

    assert D_MODEL % n_tok == 0 and (D_MODEL // n_tok) % BF16_ROWS == 0
    slab = D_MODEL // n_tok
    down_slab = BF16_ROWS * (D_FF // FF_CHUNK)
    n_down = D_FF // down_slab
    assert n_down * down_slab == D_FF and n_down <= n_tok
    ff_slab = pl.BlockSpec((slab, D_FF), lambda t: (t, 0))
    wd_slab = pl.BlockSpec((down_slab, D_MODEL), lambda t: (jnp.minimum(t, n_down - 1), 0))
    wo_slab = pl.BlockSpec((slab, D_MODEL), lambda t: (t, 0))
    x1, w_gate2, w_up2, w_down2, w_out_bf = pl.pallas_call(
        _ffn1_kernel,
        grid=(n_tok,),
        in_specs=[tok_spec] + _ffn_weight_specs() + [ff_slab, ff_slab, wd_slab, wo_slab],
        out_specs=[tok_spec, ff_slab, ff_slab, wd_slab, wo_slab],
        out_shape=[jax.ShapeDtypeStruct((T, D_MODEL), f32),
                   jax.ShapeDtypeStruct((D_MODEL, D_FF), bf16), jax.ShapeDtypeStruct((D_MODEL, D_FF), bf16),
                   jax.ShapeDtypeStruct((D_FF, D_MODEL), bf16), jax.ShapeDtypeStruct((D_MODEL, D_MODEL), bf16)],
        compiler_params=params_1d,
        name="ffn1",
    )(x.reshape(T, D_MODEL), row(ffn1_norm), ffn1_w_gate.astype(bf16), ffn1_w_up.astype(bf16),
      ffn1_w_down.astype(bf16), ffn2_w_gate, ffn2_w_up, ffn2_w_down, w_out)

    c0 = POOL_WIDTH
    w_pv = w_in[:, :c0].astype(bf16)
    w_k = w_in[:, c0 + ATTN_WIDTH:c0 + 2 * ATTN_WIDTH].astype(bf16)
    w_t = jnp.concatenate([w_in[:, c0:c0 + ATTN_WIDTH], w_in[:, c0 + 2 * ATTN_WIDTH:],
                           jnp.zeros((D_MODEL, F_ROWS - N_HEADS), w_in.dtype)], axis=1).T.astype(bf16)
    b_ft = jnp.pad(b_forget.astype(f32), (0, F_ROWS - N_HEADS)).reshape(F_ROWS, 1)
    q_gain = (jnp.tile(q_norm, N_HEADS).astype(f32) * (LOG2E / math.sqrt(HEAD_DIM))).reshape(ATTN_WIDTH, 1)
    k_gain = row(jnp.tile(k_norm, N_HEADS))
    head_id = jnp.arange(ATTN_WIDTH) // HEAD_DIM
    headsum = (head_id[:, None] == head_id[None, :]).astype(bf16)
    triu = (jnp.arange(TOKEN_TILE)[:, None] <= jnp.arange(TOKEN_TILE)[None, :]).astype(bf16)
    selk, ck = _k_aug_constants()

    seq_spec = lambda width: pl.BlockSpec((1, TOKEN_TILE, width), lambda b, s: (b, s, 0))
    y_pool, qt, qaugt, ka, vt = pl.pallas_call(
        _in_proj_kernel,
        grid=(B, S // TOKEN_TILE),
        in_specs=[seq_spec(D_MODEL), _resident((1, D_MODEL)),
                  _resident((D_MODEL, POOL_WIDTH)), _resident((D_MODEL, ATTN_WIDTH)),
                  _resident((2 * ATTN_WIDTH + F_ROWS, D_MODEL)), _resident((F_ROWS, 1)),
                  _resident((POOL_GROUPS, POOL_GROUP_DIM, POOL_GROUP_DIM)), _resident((1, POOL_WIDTH)),
                  _resident((1, POOL_WIDTH)), _resident((ATTN_WIDTH, 1)), _resident((1, ATTN_WIDTH)),
                  _resident((ATTN_WIDTH, ATTN_WIDTH)), _resident((TOKEN_TILE, TOKEN_TILE)),
                  _resident((LANES, N_HEAD_BLOCKS * LANES)), _resident((1, N_HEAD_BLOCKS * LANES))],
        out_specs=[seq_spec(POOL_WIDTH),
                   pl.BlockSpec((1, ATTN_WIDTH, TOKEN_TILE), lambda b, s: (b, 0, s)),
                   pl.BlockSpec((1, N_HEADS, BF16_ROWS, TOKEN_TILE), lambda b, s: (b, 0, 0, s)),
                   seq_spec(N_HEAD_BLOCKS * QK_DEPTH),
                   pl.BlockSpec((1, ATTN_WIDTH, TOKEN_TILE), lambda b, s: (b, 0, s))],
        out_shape=[jax.ShapeDtypeStruct((B, S, POOL_WIDTH), bf16),
                   jax.ShapeDtypeStruct((B, ATTN_WIDTH, S), bf16),
                   jax.ShapeDtypeStruct((B, N_HEADS, BF16_ROWS, S), bf16),
                   jax.ShapeDtypeStruct((B, S, N_HEAD_BLOCKS * QK_DEPTH), bf16),
                   jax.ShapeDtypeStruct((B, ATTN_WIDTH, S), bf16)],
        scratch_shapes=[pltpu.VMEM((POOL_HALO, POOL_WIDTH), f32), pltpu.VMEM((F_ROWS, LANES), f32)],
        compiler_params=params_2d,
        name="in_proj",
    )(x1.reshape(B, S, D_MODEL), row(mix_norm), w_pv, w_k, w_t, b_ft,
      pool_w.astype(bf16), row(pool_scale), row(out_norm_pool), q_gain, k_gain, headsum, triu, selk, ck)

    assert S == ATTN_TILE * sum(len(tiles) for tiles in ATTN_TILE_GROUPS)
    seq_rows = lambda width: pl.BlockSpec((1, S, width), lambda b, g: (b, 0, 0))
    seq_lanes = lambda rows: pl.BlockSpec((1, rows, S), lambda b, g: (b, 0, 0))
    o_attn = pl.pallas_call(
        _attn_kernel,
        grid=(B, len(ATTN_TILE_GROUPS)),
        in_specs=[seq_lanes(ATTN_WIDTH), pl.BlockSpec((1, N_HEADS, BF16_ROWS, S), lambda b, g: (b, 0, 0, 0)),
                  seq_rows(N_HEAD_BLOCKS * QK_DEPTH), seq_lanes(ATTN_WIDTH)],
        out_specs=seq_rows(ATTN_WIDTH),
        out_shape=jax.ShapeDtypeStruct((B, S, ATTN_WIDTH), f32),
        scratch_shapes=[pltpu.VMEM((N_HEADS, LANES, S), bf16),
                        pltpu.VMEM((max(len(tiles) for tiles in ATTN_TILE_GROUPS), N_HEADS, QK_DEPTH, ATTN_TILE),
                                   bf16),
                        pltpu.VMEM((2, 2, N_HEADS, ATTN_TILE, ATTN_TILE), f32),
                        pltpu.VMEM((2, N_HEADS, 1, ATTN_TILE), f32),
                        pltpu.VMEM((2, N_HEADS, LANES, ATTN_TILE), f32)],
        compiler_params=params_2d,
        name="attention",
    )(qt, qaugt, ka, vt)

    half_spec = pl.BlockSpec((ffn_tile, POOL_WIDTH), lambda t: (t, 0))
    out = pl.pallas_call(
        _mix_ffn2_kernel,
        grid=(n_tok,),
        in_specs=[tok_spec, half_spec, half_spec, _resident((1, ATTN_WIDTH)),
                  _resident((D_MODEL, D_MODEL))] + _ffn_weight_specs(),
        out_specs=tok_spec,
        out_shape=jax.ShapeDtypeStruct((T, D_MODEL), f32),
        compiler_params=params_1d,
        name="mix_ffn2",
    )(x1, y_pool.reshape(T, POOL_WIDTH), o_attn.reshape(T, ATTN_WIDTH), row(out_norm_attn),
      w_out_bf, row(ffn2_norm), w_gate2, w_up2, w_down2)
    return out.reshape(B, S, D_MODEL)
```
